```python
import jax, jax.numpy as jnp
from jax import lax
import numpy as np

D_MODEL = 1024
BATCH = 16
SEQ = 2048
DEPTH = 1

CHUNK = 64
Q_BLOCK = 128
N_MEM = 256
RMS_EPS = 1e-6
FFN_HIDDEN = 2816
MLA_HEADS = 8
Q_LORA = 384
KV_LORA = 256
NOPE_DIM = 128
ROPE_DIM = 64
V_DIM = 128
ROPE_THETA = 10000.0
RNN_WIDTH = 1024
RNN_BLOCKS = 8
RNN_BLOCK_DIM = RNN_WIDTH // RNN_BLOCKS
CONV_WIDTH = 4
LRU_C = 8.0
MEM_HEADS = 4
MEM_HEAD_DIM = 256
MEM_WIDTH = MEM_HEADS * MEM_HEAD_DIM
N_BRANCH = 3
BRANCH_WIDTH = MLA_HEADS * V_DIM
IN_SPLITS = (Q_LORA, KV_LORA, ROPE_DIM, RNN_WIDTH, RNN_WIDTH, MEM_WIDTH, N_BRANCH * D_MODEL)
IN_WIDTH = sum(IN_SPLITS)
SPLIT_POINTS = [int(p) for p in np.cumsum(IN_SPLITS)[:-1]]

kernel_name = 'hybrid_mla_rglru_memxattn_macaron'


def rms_norm(x, g):
    xf = x.astype(jnp.float32)
    y = xf * lax.rsqrt(jnp.mean(xf * xf, axis=-1, keepdims=True) + RMS_EPS)
    return (y * g.astype(jnp.float32)).astype(x.dtype)


def swiglu_half_step(x, g, w_in, w_down):
    h = rms_norm(x, g)
    gate, up = jnp.split(h @ w_in, 2, axis=-1)
    return x + 0.5 * ((jax.nn.silu(gate) * up) @ w_down)


def rotary_tables(seq_len, dtype):
    pos = jnp.arange(seq_len, dtype=jnp.float32)
    inv_freq = 1.0 / (ROPE_THETA ** (jnp.arange(0, ROPE_DIM, 2, dtype=jnp.float32) / ROPE_DIM))
    ang = pos[:, None] * inv_freq[None, :]
    return jnp.cos(ang).astype(dtype), jnp.sin(ang).astype(dtype)


def rotary(x, cos, sin):
    x1, x2 = jnp.split(x, 2, axis=-1)
    return jnp.concatenate([x1 * cos - x2 * sin, x2 * cos + x1 * sin], axis=-1)


def block_causal_attention(q, k, v):
    seq = q.shape[1]
    scale = q.shape[-1] ** -0.5
    outs = []
    for blk in range(seq // Q_BLOCK):
        q_lo, q_hi = blk * Q_BLOCK, (blk + 1) * Q_BLOCK
        qb = q[:, q_lo:q_hi]
        kb = k[:, :q_hi]
        vb = v[:, :q_hi]
        s = jnp.einsum('bqhd,bkhd->bhqk', qb, kb).astype(jnp.float32) * scale
        q_chunk = jnp.arange(q_lo, q_hi) // CHUNK
        k_chunk = jnp.arange(q_hi) // CHUNK
        s = jnp.where(k_chunk[None, :] <= q_chunk[:, None], s, -jnp.inf)
        p = jax.nn.softmax(s, axis=-1).astype(vb.dtype)
        outs.append(jnp.einsum('bhqk,bkhd->bqhd', p, vb))
    return jnp.concatenate(outs, axis=1)


def mla_branch(z_cq, z_ckv, z_kr, q_norm, w_uq, kv_norm, w_ukv):
    bsz, seq = z_cq.shape[:2]
    cos, sin = rotary_tables(seq, z_cq.dtype)
    q = (rms_norm(z_cq, q_norm) @ w_uq).reshape(bsz, seq, MLA_HEADS, NOPE_DIM + ROPE_DIM)
    q_nope, q_rope = jnp.split(q, [NOPE_DIM], axis=-1)
    q = jnp.concatenate([q_nope, rotary(q_rope, cos[:, None, :], sin[:, None, :])], axis=-1)
    kv = (rms_norm(z_ckv, kv_norm) @ w_ukv).reshape(bsz, seq, MLA_HEADS, NOPE_DIM + V_DIM)
    k_nope, v = jnp.split(kv, [NOPE_DIM], axis=-1)
    k_rope = rotary(z_kr, cos, sin)
    k_rope = jnp.broadcast_to(k_rope[:, :, None, :], (bsz, seq, MLA_HEADS, ROPE_DIM))
    k = jnp.concatenate([k_nope, k_rope], axis=-1)
    o = block_causal_attention(q, k, v)
    return o.reshape(bsz, seq, MLA_HEADS * V_DIM)


def _linear_scan_combine(left, right):
    a_l, b_l = left
    a_r, b_r = right
    return a_l * a_r, a_r * b_l + b_r


def rglru_branch(z_x, z_g, conv_w, conv_b, w_rg_a, b_rg_a, w_rg_i, b_rg_i, lru_lambda):
    bsz, seq = z_x.shape[:2]
    xc = lax.conv_general_dilated(
        z_x, conv_w, window_strides=(1,), padding=[(CONV_WIDTH - 1, 0)],
        dimension_numbers=('NWC', 'WIO', 'NWC'), feature_group_count=RNN_WIDTH) + conv_b
    xr = xc.reshape(bsz, seq, RNN_BLOCKS, RNN_BLOCK_DIM)
    r = jax.nn.sigmoid(jnp.einsum('bsnd,nde->bsne', xr, w_rg_a) + b_rg_a).reshape(bsz, seq, RNN_WIDTH)
    i = jax.nn.sigmoid(jnp.einsum('bsnd,nde->bsne', xr, w_rg_i) + b_rg_i).reshape(bsz, seq, RNN_WIDTH)
    log_a = -LRU_C * r.astype(jnp.float32) * jax.nn.softplus(-lru_lambda.astype(jnp.float32))
    a = jnp.exp(log_a)
    u = jnp.sqrt(-jnp.expm1(2.0 * log_a)) * (i * xc).astype(jnp.float32)
    _, h = lax.associative_scan(_linear_scan_combine, (a, u), axis=1)
    return h.astype(z_x.dtype) * jax.nn.gelu(z_g)


def memory_branch(z_mq, mem, mem_norm, w_mem_kv):
    bsz, seq = z_mq.shape[:2]
    q = z_mq.reshape(bsz, seq, MEM_HEADS, MEM_HEAD_DIM)
    k, v = jnp.split(rms_norm(mem, mem_norm) @ w_mem_kv, 2, axis=-1)
    k = k.reshape(bsz, -1, MEM_HEADS, MEM_HEAD_DIM)
    v = v.reshape(bsz, -1, MEM_HEADS, MEM_HEAD_DIM)
    s = jnp.einsum('bshd,bmhd->bhsm', q, k).astype(jnp.float32) * (MEM_HEAD_DIM ** -0.5)
    p = jax.nn.softmax(s, axis=-1).astype(v.dtype)
    return jnp.einsum('bhsm,bmhd->bshd', p, v).reshape(bsz, seq, MEM_WIDTH)


def setup_inputs(seed: int = 0) -> dict:
    key = jax.random.key(seed)
    ks = jax.random.split(key, 32)
    L = DEPTH

    def dense(k, shape, fan_in):
        return jax.random.normal(k, shape, jnp.float32) * fan_in ** -0.5

    def gain(k, shape):
        return 1.0 + 0.02 * jax.random.normal(k, shape, jnp.float32)

    def small(k, shape):
        return 0.01 * jax.random.normal(k, shape, jnp.float32)

    a_c = jax.random.uniform(ks[18], (L, RNN_WIDTH), jnp.float32, 0.9, 0.999)
    base = a_c ** (1.0 / LRU_C)
    lru_lambda = jnp.log(base) - jnp.log1p(-base)
    return {
        'x': jax.random.normal(ks[0], (BATCH, SEQ, D_MODEL), jnp.float32),
        'mem': jax.random.normal(ks[1], (BATCH, N_MEM, D_MODEL), jnp.float32),
        'ffn1_norm': gain(ks[2], (L, D_MODEL)),
        'ffn1_w_in': dense(ks[3], (L, D_MODEL, 2 * FFN_HIDDEN), D_MODEL),
        'ffn1_w_down': dense(ks[4], (L, FFN_HIDDEN, D_MODEL), FFN_HIDDEN),
        'mix_norm': gain(ks[5], (L, D_MODEL)),
        'w_in': dense(ks[6], (L, D_MODEL, IN_WIDTH), D_MODEL),
        'b_gate': small(ks[7], (L, N_BRANCH, D_MODEL)),
        'q_norm': gain(ks[8], (L, Q_LORA)),
        'w_uq': dense(ks[9], (L, Q_LORA, MLA_HEADS * (NOPE_DIM + ROPE_DIM)), Q_LORA),
        'kv_norm': gain(ks[10], (L, KV_LORA)),
        'w_ukv': dense(ks[11], (L, KV_LORA, MLA_HEADS * (NOPE_DIM + V_DIM)), KV_LORA),
        'conv_w': dense(ks[12], (L, CONV_WIDTH, 1, RNN_WIDTH), CONV_WIDTH),
        'conv_b': small(ks[13], (L, RNN_WIDTH)),
        'w_rg_a': dense(ks[14], (L, RNN_BLOCKS, RNN_BLOCK_DIM, RNN_BLOCK_DIM), RNN_BLOCK_DIM),
        'b_rg_a': small(ks[15], (L, RNN_BLOCKS, RNN_BLOCK_DIM)),
        'w_rg_i': dense(ks[16], (L, RNN_BLOCKS, RNN_BLOCK_DIM, RNN_BLOCK_DIM), RNN_BLOCK_DIM),
        'b_rg_i': small(ks[17], (L, RNN_BLOCKS, RNN_BLOCK_DIM)),
        'lru_lambda': lru_lambda,
        'mem_norm': gain(ks[19], (L, D_MODEL)),
        'w_mem_kv': dense(ks[20], (L, D_MODEL, 2 * MEM_WIDTH), D_MODEL),
        'w_branch': dense(ks[21], (L, N_BRANCH, BRANCH_WIDTH, D_MODEL), BRANCH_WIDTH),
        'w_out': dense(ks[22], (L, D_MODEL, D_MODEL), D_MODEL),
        'ffn2_norm': gain(ks[23], (L, D_MODEL)),
        'ffn2_w_in': dense(ks[24], (L, D_MODEL, 2 * FFN_HIDDEN), D_MODEL),
        'ffn2_w_down': dense(ks[25], (L, FFN_HIDDEN, D_MODEL), FFN_HIDDEN),
        'final_norm': gain(ks[26], (D_MODEL,)),
    }


def reference(x, mem, ffn1_norm, ffn1_w_in, ffn1_w_down, mix_norm, w_in, b_gate,
              q_norm, w_uq, kv_norm, w_ukv, conv_w, conv_b, w_rg_a, b_rg_a,
              w_rg_i, b_rg_i, lru_lambda, mem_norm, w_mem_kv, w_branch, w_out,
              ffn2_norm, ffn2_w_in, ffn2_w_down, final_norm):
    bsz, seq = x.shape[:2]
    for l in range(DEPTH):
        x = swiglu_half_step(x, ffn1_norm[l], ffn1_w_in[l], ffn1_w_down[l])
        h = rms_norm(x, mix_norm[l])
        z = h @ w_in[l]
        z_cq, z_ckv, z_kr, z_x, z_g, z_mq, z_gate = jnp.split(z, SPLIT_POINTS, axis=-1)
        y_a = mla_branch(z_cq, z_ckv, z_kr, q_norm[l], w_uq[l], kv_norm[l], w_ukv[l])
        y_b = rglru_branch(z_x, z_g, conv_w[l], conv_b[l], w_rg_a[l], b_rg_a[l],
                           w_rg_i[l], b_rg_i[l], lru_lambda[l])
        y_c = memory_branch(z_mq, mem, mem_norm[l], w_mem_kv[l])
        branches = jnp.stack([y_a, y_b, y_c], axis=2)
        gates = jax.nn.sigmoid(z_gate.reshape(bsz, seq, N_BRANCH, D_MODEL) + b_gate[l])
        proj = jnp.einsum('bsnc,ncd->bsnd', branches, w_branch[l])
        merged = jnp.sum(gates * proj, axis=2)
        x = x + merged @ w_out[l]
        x = swiglu_half_step(x, ffn2_norm[l], ffn2_w_in[l], ffn2_w_down[l])
    return rms_norm(x, final_norm)
```

```python
import functools

import jax
import jax.numpy as jnp
import numpy as np
from jax import lax
from jax.experimental import pallas as pl
from jax.experimental.pallas import tpu as pltpu

F32 = jnp.float32
BF16 = jnp.bfloat16

D_MODEL = 1024
CHUNK = 64
N_MEM = 256
RMS_EPS = 1e-6
FFN_HIDDEN = 2816
MLA_HEADS = 8
Q_LORA = 384
KV_LORA = 256
NOPE_DIM = 128
ROPE_DIM = 64
V_DIM = 128
ROPE_THETA = 10000.0
RNN_WIDTH = 1024
RNN_BLOCKS = 8
RNN_BLOCK_DIM = RNN_WIDTH // RNN_BLOCKS
CONV_WIDTH = 4
LRU_C = 8.0
MEM_HEADS = 4
MEM_HEAD_DIM = 256
MEM_WIDTH = MEM_HEADS * MEM_HEAD_DIM
N_BRANCH = 3

HALF_ROPE = ROPE_DIM // 2
QK_PAD = 256
LAT_WIDTH = 768
HALO = 8

VMEM_LIMIT_BYTES = 56 * 1024 * 1024

FFN_TOKENS = 512
FFN_CHUNKS = ((0, 1024), (1024, 2048), (2048, FFN_HIDDEN))
INPROJ_TOKENS = 256
ATTN_Q = 256
ATTN_K = 256
SCAN_TOKENS = 256
MERGE_TOKENS = 256


def _dot(a, b):
    return jnp.dot(a, b, preferred_element_type=F32)


def _dot_nt(a, b):
    return lax.dot_general(a, b, (((1,), (1,)), ((), ())), preferred_element_type=F32)


def _rms(x, g):
    return x * lax.rsqrt(jnp.mean(x * x, axis=-1, keepdims=True) + RMS_EPS) * g


def _const_spec(shape):
    nd = len(shape)
    return pl.BlockSpec(shape, lambda *_: (0,) * nd, pipeline_mode=pl.Buffered(1))


def _params(*sem):
    return pltpu.CompilerParams(dimension_semantics=sem, vmem_limit_bytes=VMEM_LIMIT_BYTES)


def _ffn_body(x_ref, g_ref, wg_ref, wu_ref, wd_ref, fn_ref, o_ref, *, final):
    x = x_ref[...]
    h = _rms(x, g_ref[...]).astype(BF16)
    acc = None
    for lo, hi in FFN_CHUNKS:
        gate = _dot(h, wg_ref[:, lo:hi])
        up = _dot(h, wu_ref[:, lo:hi])
        a = (gate * jax.nn.sigmoid(gate) * up).astype(BF16)
        part = _dot(a, wd_ref[lo:hi, :])
        acc = part if acc is None else acc + part
    y = x + 0.5 * acc
    if final:
        y = _rms(y, fn_ref[...])
    o_ref[...] = y


def _ffn(x, g, wg, wu, wd, fn, *, final):
    t = x.shape[0]
    tm = FFN_TOKENS
    row = pl.BlockSpec((tm, D_MODEL), lambda i: (i, 0))
    return pl.pallas_call(
        functools.partial(_ffn_body, final=final),
        grid=(t // tm,),
        in_specs=[row, _const_spec((1, D_MODEL)), _const_spec(wg.shape), _const_spec(wu.shape),
                  _const_spec(wd.shape), _const_spec((1, D_MODEL))],
        out_specs=row,
        out_shape=jax.ShapeDtypeStruct((t, D_MODEL), F32),
        compiler_params=_params("parallel"),
        name="ffn_final" if final else "ffn",
    )(x, g, wg, wu, wd, fn)


def _gelu_tanh(x):
    return 0.5 * x * (1.0 + jnp.tanh(np.sqrt(2.0 / np.pi) * (x + 0.044715 * (x * x * x))))


def _rot(x, c, s_lo, s_hi):
    w = x.shape[-1]
    return x * c + pltpu.roll(x, w - HALF_ROPE, 1) * s_lo + pltpu.roll(x, HALF_ROPE, 1) * s_hi


def _inproj_body(x_ref, gmix_ref, wlat_ref, wx_ref, wg_ref, wmq_ref, wgate_ref, bgate_ref,
                 qnorm_ref, wuq_ref, kvnorm_ref, wuk_ref, wuv_ref,
                 cq_ref, sq_lo_ref, sq_hi_ref, ck_ref, sk_lo_ref, sk_hi_ref,
                 q_out, kn_out, kr_out, v_out, zx_out, gz_out, mq_out, gate_out):
    h = _rms(x_ref[...], gmix_ref[...]).astype(BF16)
    lat = _dot(h, wlat_ref[...])
    hq = _rms(lat[:, :Q_LORA], qnorm_ref[...]).astype(BF16)
    hkv = _rms(lat[:, Q_LORA:Q_LORA + KV_LORA], kvnorm_ref[...]).astype(BF16)
    zkr = lat[:, Q_LORA + KV_LORA:]

    q = _dot(hq, wuq_ref[...])
    cq, sq_lo, sq_hi = cq_ref[...], sq_lo_ref[...], sq_hi_ref[...]
    for hd in range(MLA_HEADS):
        sl = slice(hd * QK_PAD, (hd + 1) * QK_PAD)
        q_out[:, sl] = _rot(q[:, sl], cq, sq_lo, sq_hi).astype(BF16)
    kn_out[...] = _dot(hkv, wuk_ref[...]).astype(BF16)
    v_out[...] = _dot(hkv, wuv_ref[...]).astype(BF16)
    kr_out[...] = _rot(zkr, ck_ref[...], sk_lo_ref[...], sk_hi_ref[...]).astype(BF16)

    zx_out[...] = _dot(h, wx_ref[...])
    gz_out[...] = _gelu_tanh(_dot(h, wg_ref[...])).astype(BF16)
    mq_out[...] = (_dot(h, wmq_ref[...]) * (MEM_HEAD_DIM ** -0.5)).astype(BF16)
    gate_out[...] = jax.nn.sigmoid(_dot(h, wgate_ref[...]) + bgate_ref[...]).astype(BF16)


def _inproj(x, seq, gmix, wlat, wx, wg, wmq, wgate, bgate, qnorm, wuq, kvnorm, wuk, wuv, tables):
    t = x.shape[0]
    tm = INPROJ_TOKENS
    n_s = seq // tm
    bsz = t // seq

    def row(width):
        return pl.BlockSpec((tm, width), lambda i, b: (b * n_s + i, 0))

    def tab(width):
        return pl.BlockSpec((tm, width), lambda i, b: (i, 0))

    consts = [gmix, wlat, wx, wg, wmq, wgate, bgate, qnorm, wuq, kvnorm, wuk, wuv]
    out_widths = [(MLA_HEADS * QK_PAD, BF16), (MLA_HEADS * NOPE_DIM, BF16), (128, BF16), (MLA_HEADS * V_DIM, BF16),
                  (RNN_WIDTH, F32), (RNN_WIDTH, BF16), (MEM_WIDTH, BF16), (N_BRANCH * D_MODEL, BF16)]
    return pl.pallas_call(
        _inproj_body,
        grid=(n_s, bsz),
        in_specs=[row(D_MODEL)] + [_const_spec(c.shape) for c in consts]
                 + [tab(QK_PAD)] * 3 + [tab(128)] * 3,
        out_specs=[row(w) for w, _ in out_widths],
        out_shape=[jax.ShapeDtypeStruct((t, w), dt) for w, dt in out_widths],
        compiler_params=_params("parallel", "parallel"),
        name="inproj",
    )(x, *consts, *tables)


def _attn_body(q_ref, kn_ref, kr_ref, v_ref, o_ref):
    i = pl.program_id(1)
    row_chunk = lax.broadcasted_iota(jnp.int32, (ATTN_Q, ATTN_K), 0) // CHUNK
    col_chunk = lax.broadcasted_iota(jnp.int32, (ATTN_Q, ATTN_K), 1) // CHUNK
    diag_mask = col_chunk <= row_chunk

    for hd in range(MLA_HEADS):
        q = q_ref[:, hd * QK_PAD:(hd + 1) * QK_PAD]
        nope = slice(hd * NOPE_DIM, (hd + 1) * NOPE_DIM)
        vcol = slice(hd * V_DIM, (hd + 1) * V_DIM)

        def step(j, carry, masked):
            m, l, acc = carry
            off = pl.multiple_of(j * ATTN_K, ATTN_K)
            k = jnp.concatenate([kn_ref[pl.ds(off, ATTN_K), nope], kr_ref[pl.ds(off, ATTN_K), :]], axis=1)
            v = v_ref[pl.ds(off, ATTN_K), vcol]
            s = _dot_nt(q, k)
            if masked:
                s = jnp.where(diag_mask, s, -jnp.inf)
            m_new = jnp.maximum(m, jnp.max(s, axis=-1, keepdims=True))
            alpha = jnp.exp(m - m_new)
            p = jnp.exp(s - m_new)
            l = alpha * l + jnp.sum(p, axis=-1, keepdims=True)
            acc = alpha * acc + _dot(p.astype(BF16), v)
            return m_new, l, acc

        init = (jnp.full((ATTN_Q, 1), -jnp.inf, F32), jnp.zeros((ATTN_Q, 1), F32), jnp.zeros((ATTN_Q, V_DIM), F32))
        carry = lax.fori_loop(0, i, functools.partial(step, masked=False), init)
        _, l, acc = step(i, carry, True)
        o_ref[:, vcol] = (acc / l).astype(BF16)


def _mla_attention(q, kn, kr, v, seq):
    t = q.shape[0]
    bsz = t // seq
    n_q = seq // ATTN_Q

    def full(width):
        return pl.BlockSpec((seq, width), lambda b, i: (b, 0))

    return pl.pallas_call(
        _attn_body,
        grid=(bsz, n_q),
        in_specs=[pl.BlockSpec((ATTN_Q, MLA_HEADS * QK_PAD), lambda b, i: (b * n_q + i, 0)),
                  full(MLA_HEADS * NOPE_DIM), full(128), full(MLA_HEADS * V_DIM)],
        out_specs=pl.BlockSpec((ATTN_Q, MLA_HEADS * V_DIM), lambda b, i: (b * n_q + i, 0)),
        out_shape=jax.ShapeDtypeStruct((t, MLA_HEADS * V_DIM), BF16),
        compiler_params=_params("parallel", "parallel"),
        name="mla_attn",
    )(q, kn, kr, v)


def _rglru_body(zx_ref, gz_ref, cw_ref, cb_ref, wai_ref, bai_ref, lam_ref, y_ref, xbuf, hstate, a_s, u_s):
    ts = SCAN_TOKENS

    @pl.when(pl.program_id(1) == 0)
    def _():
        xbuf[0:HALO, :] = jnp.zeros((HALO, RNN_WIDTH), F32)
        hstate[...] = jnp.zeros((1, RNN_WIDTH), F32)

    xbuf[HALO:HALO + ts, :] = zx_ref[...]
    xc = cb_ref[...] + cw_ref[CONV_WIDTH - 1:CONV_WIDTH, :] * xbuf[HALO:HALO + ts, :]
    for d in range(1, CONV_WIDTH):
        w = cw_ref[CONV_WIDTH - 1 - d:CONV_WIDTH - d, :]
        xc = xc + w * xbuf[HALO - d:HALO - d + ts, :]
    xbuf[0:HALO, :] = xbuf[ts:ts + HALO, :]

    xcb = xc.astype(BF16)
    lam = lam_ref[...]
    sp = jnp.maximum(-lam, 0.0) + jnp.log1p(jnp.exp(-jnp.abs(lam)))
    for n in range(RNN_BLOCKS):
        blk = slice(n * RNN_BLOCK_DIM, (n + 1) * RNN_BLOCK_DIM)
        g = _dot(xcb[:, blk], wai_ref[n]) + bai_ref[n]
        r = jax.nn.sigmoid(g[:, :RNN_BLOCK_DIM])
        ig = jax.nn.sigmoid(g[:, RNN_BLOCK_DIM:])
        log_a = (-LRU_C) * r * sp[:, blk]
        a = jnp.exp(log_a)
        u = jnp.sqrt(-jnp.tanh(log_a) * (1.0 + a * a)) * (ig * xc[:, blk])
        a_s[:, blk] = a
        u_s[:, blk] = u

    def scan_step(t, h):
        h = a_s[pl.ds(t, 1), :] * h + u_s[pl.ds(t, 1), :]
        u_s[pl.ds(t, 1), :] = h
        return h

    hstate[...] = lax.fori_loop(0, ts, scan_step, hstate[...], unroll=8)
    y_ref[...] = (u_s[...] * gz_ref[...].astype(F32)).astype(BF16)


def _rglru(zx, gz, cw, cb, wai, bai, lam, seq):
    t = zx.shape[0]
    ts = SCAN_TOKENS
    n_s = seq // ts
    bsz = t // seq
    row = pl.BlockSpec((ts, RNN_WIDTH), lambda b, i: (b * n_s + i, 0))
    return pl.pallas_call(
        _rglru_body,
        grid=(bsz, n_s),
        in_specs=[row, row, _const_spec(cw.shape), _const_spec(cb.shape), _const_spec(wai.shape),
                  _const_spec(bai.shape), _const_spec(lam.shape)],
        out_specs=row,
        out_shape=jax.ShapeDtypeStruct((t, RNN_WIDTH), BF16),
        scratch_shapes=[pltpu.VMEM((HALO + ts, RNN_WIDTH), F32), pltpu.VMEM((1, RNN_WIDTH), F32),
                        pltpu.VMEM((ts, RNN_WIDTH), F32), pltpu.VMEM((ts, RNN_WIDTH), F32)],
        compiler_params=_params("parallel", "arbitrary"),
        name="rglru",
    )(zx, gz, cw, cb, wai, bai, lam)


def _memkv_body(m_ref, g_ref, wk_ref, wv_ref, k_out, v_out):
    h = _rms(m_ref[...], g_ref[...]).astype(BF16)
    k_out[...] = _dot(h, wk_ref[...]).astype(BF16)
    v_out[...] = _dot(h, wv_ref[...]).astype(BF16)


def _memkv(mem, g, wk, wv):
    t = mem.shape[0]
    row = pl.BlockSpec((N_MEM, D_MODEL), lambda b: (b, 0))
    out = pl.BlockSpec((N_MEM, MEM_WIDTH), lambda b: (b, 0))
    return pl.pallas_call(
        _memkv_body,
        grid=(t // N_MEM,),
        in_specs=[row, _const_spec(g.shape), _const_spec(wk.shape), _const_spec(wv.shape)],
        out_specs=[out, out],
        out_shape=[jax.ShapeDtypeStruct((t, MEM_WIDTH), BF16)] * 2,
        compiler_params=_params("parallel"),
        name="memkv",
    )(mem, g, wk, wv)


def _merge_body(x_ref, ya_ref, yb_ref, mq_ref, gate_ref, mk_ref, mv_ref, wb_ref, wo_ref, o_ref):
    ycs = []
    for hd in range(MEM_HEADS):
        sl = slice(hd * MEM_HEAD_DIM, (hd + 1) * MEM_HEAD_DIM)
        s = _dot_nt(mq_ref[:, sl], mk_ref[:, sl])
        p = jnp.exp(s - jnp.max(s, axis=-1, keepdims=True))
        l = jnp.sum(p, axis=-1, keepdims=True)
        ycs.append((_dot(p.astype(BF16), mv_ref[:, sl]) / l).astype(BF16))
    yc = jnp.concatenate(ycs, axis=1)

    merged = None
    for n, y in enumerate((ya_ref[...], yb_ref[...], yc)):
        gate = gate_ref[:, n * D_MODEL:(n + 1) * D_MODEL].astype(F32)
        term = gate * _dot(y, wb_ref[n])
        merged = term if merged is None else merged + term
    o_ref[...] = x_ref[...] + _dot(merged.astype(BF16), wo_ref[...])


def _merge(x, ya, yb, mq, gates, mk, mv, wb, wo, seq):
    t = x.shape[0]
    tm = MERGE_TOKENS
    n_s = seq // tm

    def row(width):
        return pl.BlockSpec((tm, width), lambda i: (i, 0))

    memspec = pl.BlockSpec((N_MEM, MEM_WIDTH), lambda i: (i // n_s, 0))
    return pl.pallas_call(
        _merge_body,
        grid=(t // tm,),
        in_specs=[row(D_MODEL), row(D_MODEL), row(D_MODEL), row(MEM_WIDTH), row(N_BRANCH * D_MODEL),
                  memspec, memspec, _const_spec(wb.shape), _const_spec(wo.shape)],
        out_specs=row(D_MODEL),
        out_shape=jax.ShapeDtypeStruct((t, D_MODEL), F32),
        compiler_params=_params("parallel"),
        name="merge",
    )(x, ya, yb, mq, gates, mk, mv, wb, wo)


def _rotary_tables(seq):
    pos = jnp.arange(seq, dtype=F32)
    inv_freq = 1.0 / (ROPE_THETA ** (jnp.arange(0, ROPE_DIM, 2, dtype=F32) / ROPE_DIM))
    ang = pos[:, None] * inv_freq[None, :]
    cos, sin = jnp.cos(ang), jnp.sin(ang)
    zero = jnp.zeros_like(cos)

    def lay(lead, a, b, width):
        parts = ([lead] if lead is not None else []) + [a, b]
        used = sum(p.shape[1] for p in parts)
        return jnp.concatenate(parts + [jnp.zeros((seq, width - used), F32)], axis=1)

    scale = (NOPE_DIM + ROPE_DIM) ** -0.5
    one = jnp.ones((seq, NOPE_DIM), F32)
    lead0 = jnp.zeros((seq, NOPE_DIM), F32)
    q_tabs = (lay(one, cos, cos, QK_PAD) * scale, lay(lead0, -sin, zero, QK_PAD) * scale,
              lay(lead0, zero, sin, QK_PAD) * scale)
    k_tabs = (lay(None, cos, cos, 128), lay(None, -sin, zero, 128), lay(None, zero, sin, 128))
    return q_tabs + k_tabs


def _pad_heads(w, widths, pad_to):
    fan_in = w.shape[0]
    per = sum(widths)
    w = w.reshape(fan_in, -1, per)[:, :, :widths[0]]
    w = jnp.pad(w, ((0, 0), (0, 0), (0, pad_to - widths[0])))
    return w.reshape(fan_in, -1)


def kernel(x, mem, ffn1_norm, ffn1_w_in, ffn1_w_down, mix_norm, w_in, b_gate, q_norm, w_uq, kv_norm, w_ukv,
           conv_w, conv_b, w_rg_a, b_rg_a, w_rg_i, b_rg_i, lru_lambda, mem_norm, w_mem_kv, w_branch, w_out,
           ffn2_norm, ffn2_w_in, ffn2_w_down, final_norm):
    bsz, seq, _ = x.shape
    depth = ffn1_norm.shape[0]
    xt = x.reshape(bsz * seq, D_MODEL)
    memt = mem.reshape(bsz * N_MEM, D_MODEL)
    tables = _rotary_tables(seq)
    fn = final_norm.reshape(1, D_MODEL)
    bf = lambda a: a.astype(BF16)

    for l in range(depth):
        last = l == depth - 1
        xt = _ffn(xt, ffn1_norm[l].reshape(1, -1), bf(ffn1_w_in[l][:, :FFN_HIDDEN]), bf(ffn1_w_in[l][:, FFN_HIDDEN:]),
                  bf(ffn1_w_down[l]), fn, final=False)

        w = w_in[l]
        c0 = Q_LORA + KV_LORA + ROPE_DIM
        wlat = bf(jnp.pad(w[:, :c0], ((0, 0), (0, LAT_WIDTH - c0))))
        wx = bf(w[:, c0:c0 + RNN_WIDTH])
        wg = bf(w[:, c0 + RNN_WIDTH:c0 + 2 * RNN_WIDTH])
        c1 = c0 + 2 * RNN_WIDTH
        wmq = bf(w[:, c1:c1 + MEM_WIDTH])
        wgate = bf(w[:, c1 + MEM_WIDTH:])
        wuq = bf(_pad_heads(w_uq[l], (NOPE_DIM + ROPE_DIM,), QK_PAD))
        wuk = bf(_pad_heads(w_ukv[l], (NOPE_DIM, V_DIM), NOPE_DIM))
        wuv = bf(w_ukv[l].reshape(KV_LORA, MLA_HEADS, NOPE_DIM + V_DIM)[:, :, NOPE_DIM:].reshape(KV_LORA, -1))

        q, kn, kr, v, zx, gz, mq, gates = _inproj(
            xt, seq, mix_norm[l].reshape(1, -1), wlat, wx, wg, wmq, wgate, b_gate[l].reshape(1, -1),
            q_norm[l].reshape(1, -1), wuq, kv_norm[l].reshape(1, -1), wuk, wuv, tables)

        ya = _mla_attention(q, kn, kr, v, seq)

        wai = bf(jnp.concatenate([w_rg_a[l], w_rg_i[l]], axis=-1))
        bai = jnp.concatenate([b_rg_a[l], b_rg_i[l]], axis=-1).reshape(RNN_BLOCKS, 1, 2 * RNN_BLOCK_DIM)
        yb = _rglru(zx, gz, conv_w[l].reshape(CONV_WIDTH, RNN_WIDTH), conv_b[l].reshape(1, -1), wai, bai,
                    lru_lambda[l].reshape(1, -1), seq)

        mk, mv = _memkv(memt, mem_norm[l].reshape(1, -1), bf(w_mem_kv[l][:, :MEM_WIDTH]),
                        bf(w_mem_kv[l][:, MEM_WIDTH:]))
        xt = _merge(xt, ya, yb, mq, gates, mk, mv, bf(w_branch[l]), bf(w_out[l]), seq)

        xt = _ffn(xt, ffn2_norm[l].reshape(1, -1), bf(ffn2_w_in[l][:, :FFN_HIDDEN]), bf(ffn2_w_in[l][:, FFN_HIDDEN:]),
                  bf(ffn2_w_down[l]), fn, final=last)
    return xt.reshape(bsz, seq, D_MODEL)
```

```python
import functools

import jax
import jax.numpy as jnp
import numpy as np
from jax import lax
from jax.experimental import pallas as pl
from jax.experimental.pallas import tpu as pltpu

F32 = jnp.float32
BF16 = jnp.bfloat16

D_MODEL = 1024
CHUNK = 64
N_MEM = 256
RMS_EPS = 1e-6
FFN_HIDDEN = 2816
MLA_HEADS = 8
Q_LORA = 384
KV_LORA = 256
NOPE_DIM = 128
ROPE_DIM = 64
V_DIM = 128
ROPE_THETA = 10000.0
RNN_WIDTH = 1024
RNN_BLOCKS = 8
RNN_BLOCK_DIM = RNN_WIDTH // RNN_BLOCKS
CONV_WIDTH = 4
LRU_C = 8.0
MEM_HEADS = 4
MEM_HEAD_DIM = 256
MEM_WIDTH = MEM_HEADS * MEM_HEAD_DIM
N_BRANCH = 3

HALF_ROPE = ROPE_DIM // 2
QK_PAD = 256
LAT_WIDTH = 768
HALO = 8

VMEM_LIMIT_BYTES = 56 * 1024 * 1024

FFN_TOKENS = 512
FFN_CHUNKS = ((0, 1024), (1024, 2048), (2048, FFN_HIDDEN))
INPROJ_TOKENS = 256
ATTN_Q = 512
SCAN_TOKENS = 256
MERGE_TOKENS = 256


def _dot(a, b):
    return jnp.dot(a, b, preferred_element_type=F32)


def _dot_nt(a, b):
    return lax.dot_general(a, b, (((1,), (1,)), ((), ())), preferred_element_type=F32)


def _rms(x, g):
    return x * lax.rsqrt(jnp.mean(x * x, axis=-1, keepdims=True) + RMS_EPS) * g


def _const_spec(shape):
    nd = len(shape)
    return pl.BlockSpec(shape, lambda *_: (0,) * nd, pipeline_mode=pl.Buffered(1))


def _params(*sem):
    return pltpu.CompilerParams(dimension_semantics=sem, vmem_limit_bytes=VMEM_LIMIT_BYTES)


def _ffn_body(x_ref, g_ref, wg_ref, wu_ref, wd_ref, fn_ref, o_ref, *, final):
    x = x_ref[...]
    h = _rms(x, g_ref[...]).astype(BF16)
    acc = None
    for lo, hi in FFN_CHUNKS:
        gate = _dot(h, wg_ref[:, lo:hi])
        up = _dot(h, wu_ref[:, lo:hi])
        a = (gate * jax.nn.sigmoid(gate) * up).astype(BF16)
        part = _dot(a, wd_ref[lo:hi, :])
        acc = part if acc is None else acc + part
    y = x + 0.5 * acc
    if final:
        y = _rms(y, fn_ref[...])
    o_ref[...] = y


def _ffn(x, g, wg, wu, wd, fn, *, final):
    t = x.shape[0]
    tm = FFN_TOKENS
    row = pl.BlockSpec((tm, D_MODEL), lambda i: (i, 0))
    return pl.pallas_call(
        functools.partial(_ffn_body, final=final),
        grid=(t // tm,),
        in_specs=[row, _const_spec((1, D_MODEL)), _const_spec(wg.shape), _const_spec(wu.shape),
                  _const_spec(wd.shape), _const_spec((1, D_MODEL))],
        out_specs=row,
        out_shape=jax.ShapeDtypeStruct((t, D_MODEL), F32),
        compiler_params=_params("parallel"),
        name="ffn_final" if final else "ffn",
    )(x, g, wg, wu, wd, fn)


def _gelu_tanh(x):
    return 0.5 * x * (1.0 + jnp.tanh(np.sqrt(2.0 / np.pi) * (x + 0.044715 * (x * x * x))))


def _rot(x, c, s_lo, s_hi):
    w = x.shape[-1]
    return x * c + pltpu.roll(x, w - HALF_ROPE, 1) * s_lo + pltpu.roll(x, HALF_ROPE, 1) * s_hi


def _inproj_body(x_ref, gmix_ref, wlat_ref, wx_ref, wg_ref, wmq_ref, wgate_ref, bgate_ref,
                 qnorm_ref, wuq_ref, kvnorm_ref, wuk_ref, wuv_ref,
                 cq_ref, sq_lo_ref, sq_hi_ref, ck_ref, sk_lo_ref, sk_hi_ref,
                 q_out, kn_out, kr_out, v_out, zx_out, gz_out, mq_out, gate_out):
    h = _rms(x_ref[...], gmix_ref[...]).astype(BF16)
    lat = _dot(h, wlat_ref[...])
    hq = _rms(lat[:, :Q_LORA], qnorm_ref[...]).astype(BF16)
    hkv = _rms(lat[:, Q_LORA:Q_LORA + KV_LORA], kvnorm_ref[...]).astype(BF16)
    zkr = lat[:, Q_LORA + KV_LORA:]

    q = _dot(hq, wuq_ref[...])
    cq, sq_lo, sq_hi = cq_ref[...], sq_lo_ref[...], sq_hi_ref[...]
    for hd in range(MLA_HEADS):
        sl = slice(hd * QK_PAD, (hd + 1) * QK_PAD)
        q_out[:, sl] = _rot(q[:, sl], cq, sq_lo, sq_hi).astype(BF16)
    kn_out[...] = _dot(hkv, wuk_ref[...]).astype(BF16)
    v_out[...] = _dot(hkv, wuv_ref[...]).astype(BF16)
    kr_out[...] = _rot(zkr, ck_ref[...], sk_lo_ref[...], sk_hi_ref[...]).astype(BF16)

    zx_out[...] = _dot(h, wx_ref[...])
    gz_out[...] = _gelu_tanh(_dot(h, wg_ref[...])).astype(BF16)
    mq_out[...] = (_dot(h, wmq_ref[...]) * (MEM_HEAD_DIM ** -0.5)).astype(BF16)
    gate_out[...] = jax.nn.sigmoid(_dot(h, wgate_ref[...]) + bgate_ref[...]).astype(BF16)


def _inproj(x, seq, gmix, wlat, wx, wg, wmq, wgate, bgate, qnorm, wuq, kvnorm, wuk, wuv, tables):
    t = x.shape[0]
    tm = INPROJ_TOKENS
    n_s = seq // tm
    bsz = t // seq

    def row(width):
        return pl.BlockSpec((tm, width), lambda i, b: (b * n_s + i, 0))

    def tab(width):
        return pl.BlockSpec((tm, width), lambda i, b: (i, 0))

    consts = [gmix, wlat, wx, wg, wmq, wgate, bgate, qnorm, wuq, kvnorm, wuk, wuv]
    out_widths = [(MLA_HEADS * QK_PAD, BF16), (MLA_HEADS * NOPE_DIM, BF16), (128, BF16), (MLA_HEADS * V_DIM, BF16),
                  (RNN_WIDTH, F32), (RNN_WIDTH, BF16), (MEM_WIDTH, BF16), (N_BRANCH * D_MODEL, BF16)]
    return pl.pallas_call(
        _inproj_body,
        grid=(n_s, bsz),
        in_specs=[row(D_MODEL)] + [_const_spec(c.shape) for c in consts]
                 + [tab(QK_PAD)] * 3 + [tab(128)] * 3,
        out_specs=[row(w) for w, _ in out_widths],
        out_shape=[jax.ShapeDtypeStruct((t, w), dt) for w, dt in out_widths],
        compiler_params=_params("parallel", "parallel"),
        name="inproj",
    )(x, *consts, *tables)


def _attn_body(q_ref, kn_ref, kr_ref, v_ref, o_ref):
    seq = q_ref.shape[0]
    tq = ATTN_Q
    row_chunk = lax.broadcasted_iota(jnp.int32, (tq, tq), 0) // CHUNK
    col_chunk = lax.broadcasted_iota(jnp.int32, (tq, tq), 1) // CHUNK
    diag_mask = col_chunk <= row_chunk

    for i in range(seq // tq):
        lo, hi = i * tq, (i + 1) * tq
        q = q_ref[lo:hi, :]
        k_diag = jnp.concatenate([kn_ref[lo:hi, :], kr_ref[lo:hi, :]], axis=1)
        s_diag = jnp.where(diag_mask, _dot_nt(q, k_diag), -jnp.inf)
        m = jnp.max(s_diag, axis=-1, keepdims=True)
        if i > 0:
            k_off = jnp.concatenate([kn_ref[0:lo, :], kr_ref[0:lo, :]], axis=1)
            s_off = _dot_nt(q, k_off)
            m = jnp.maximum(m, jnp.max(s_off, axis=-1, keepdims=True))
        p_diag = jnp.exp(s_diag - m)
        l = jnp.sum(p_diag, axis=-1, keepdims=True)
        acc = _dot(p_diag.astype(BF16), v_ref[lo:hi, :])
        if i > 0:
            p_off = jnp.exp(s_off - m)
            l = l + jnp.sum(p_off, axis=-1, keepdims=True)
            acc = acc + _dot(p_off.astype(BF16), v_ref[0:lo, :])
        o_ref[lo:hi, :] = (acc / l).astype(BF16)


def _mla_attention(q, kn, kr, v, seq):
    t = q.shape[0]
    bsz = t // seq
    return pl.pallas_call(
        _attn_body,
        grid=(bsz, MLA_HEADS),
        in_specs=[pl.BlockSpec((seq, QK_PAD), lambda b, h: (b, h)),
                  pl.BlockSpec((seq, NOPE_DIM), lambda b, h: (b, h)),
                  pl.BlockSpec((seq, 128), lambda b, h: (b, 0)),
                  pl.BlockSpec((seq, V_DIM), lambda b, h: (b, h))],
        out_specs=pl.BlockSpec((seq, V_DIM), lambda b, h: (b, h)),
        out_shape=jax.ShapeDtypeStruct((t, MLA_HEADS * V_DIM), BF16),
        compiler_params=_params("parallel", "parallel"),
        name="mla_attn",
    )(q, kn, kr, v)


def _rglru_body(zx_ref, gz_ref, cw_ref, cb_ref, wai_ref, bai_ref, lam_ref, y_ref, xbuf, hstate, a_s, u_s):
    ts = SCAN_TOKENS

    @pl.when(pl.program_id(1) == 0)
    def _():
        xbuf[0:HALO, :] = jnp.zeros((HALO, RNN_WIDTH), F32)
        hstate[...] = jnp.zeros((1, RNN_WIDTH), F32)

    xbuf[HALO:HALO + ts, :] = zx_ref[...]
    xc = cb_ref[...] + cw_ref[CONV_WIDTH - 1:CONV_WIDTH, :] * xbuf[HALO:HALO + ts, :]
    for d in range(1, CONV_WIDTH):
        w = cw_ref[CONV_WIDTH - 1 - d:CONV_WIDTH - d, :]
        xc = xc + w * xbuf[HALO - d:HALO - d + ts, :]
    xbuf[0:HALO, :] = xbuf[ts:ts + HALO, :]

    xcb = xc.astype(BF16)
    lam = lam_ref[...]
    sp = jnp.maximum(-lam, 0.0) + jnp.log1p(jnp.exp(-jnp.abs(lam)))
    for n in range(RNN_BLOCKS):
        blk = slice(n * RNN_BLOCK_DIM, (n + 1) * RNN_BLOCK_DIM)
        g = _dot(xcb[:, blk], wai_ref[n]) + bai_ref[n]
        r = jax.nn.sigmoid(g[:, :RNN_BLOCK_DIM])
        ig = jax.nn.sigmoid(g[:, RNN_BLOCK_DIM:])
        log_a = (-LRU_C) * r * sp[:, blk]
        a = jnp.exp(log_a)
        u = jnp.sqrt(-jnp.tanh(log_a) * (1.0 + a * a)) * (ig * xc[:, blk])
        a_s[:, blk] = a
        u_s[:, blk] = u

    def scan_step(t, h):
        h = a_s[pl.ds(t, 1), :] * h + u_s[pl.ds(t, 1), :]
        u_s[pl.ds(t, 1), :] = h
        return h

    hstate[...] = lax.fori_loop(0, ts, scan_step, hstate[...], unroll=8)
    y_ref[...] = (u_s[...] * gz_ref[...].astype(F32)).astype(BF16)


def _rglru(zx, gz, cw, cb, wai, bai, lam, seq):
    t = zx.shape[0]
    ts = SCAN_TOKENS
    n_s = seq // ts
    bsz = t // seq
    row = pl.BlockSpec((ts, RNN_WIDTH), lambda b, i: (b * n_s + i, 0))
    return pl.pallas_call(
        _rglru_body,
        grid=(bsz, n_s),
        in_specs=[row, row, _const_spec(cw.shape), _const_spec(cb.shape), _const_spec(wai.shape),
                  _const_spec(bai.shape), _const_spec(lam.shape)],
        out_specs=row,
        out_shape=jax.ShapeDtypeStruct((t, RNN_WIDTH), BF16),
        scratch_shapes=[pltpu.VMEM((HALO + ts, RNN_WIDTH), F32), pltpu.VMEM((1, RNN_WIDTH), F32),
                        pltpu.VMEM((ts, RNN_WIDTH), F32), pltpu.VMEM((ts, RNN_WIDTH), F32)],
        compiler_params=_params("parallel", "arbitrary"),
        name="rglru",
    )(zx, gz, cw, cb, wai, bai, lam)


def _memkv_body(m_ref, g_ref, wk_ref, wv_ref, k_out, v_out):
    h = _rms(m_ref[...], g_ref[...]).astype(BF16)
    k_out[...] = _dot(h, wk_ref[...]).astype(BF16)
    v_out[...] = _dot(h, wv_ref[...]).astype(BF16)


def _memkv(mem, g, wk, wv):
    t = mem.shape[0]
    row = pl.BlockSpec((N_MEM, D_MODEL), lambda b: (b, 0))
    out = pl.BlockSpec((N_MEM, MEM_WIDTH), lambda b: (b, 0))
    return pl.pallas_call(
        _memkv_body,
        grid=(t // N_MEM,),
        in_specs=[row, _const_spec(g.shape), _const_spec(wk.shape), _const_spec(wv.shape)],
        out_specs=[out, out],
        out_shape=[jax.ShapeDtypeStruct((t, MEM_WIDTH), BF16)] * 2,
        compiler_params=_params("parallel"),
        name="memkv",
    )(mem, g, wk, wv)


def _merge_body(x_ref, ya_ref, yb_ref, mq_ref, gate_ref, mk_ref, mv_ref, wb_ref, wo_ref, o_ref):
    ycs = []
    for hd in range(MEM_HEADS):
        sl = slice(hd * MEM_HEAD_DIM, (hd + 1) * MEM_HEAD_DIM)
        s = _dot_nt(mq_ref[:, sl], mk_ref[:, sl])
        p = jnp.exp(s - jnp.max(s, axis=-1, keepdims=True))
        l = jnp.sum(p, axis=-1, keepdims=True)
        ycs.append((_dot(p.astype(BF16), mv_ref[:, sl]) / l).astype(BF16))
    yc = jnp.concatenate(ycs, axis=1)

    merged = None
    for n, y in enumerate((ya_ref[...], yb_ref[...], yc)):
        gate = gate_ref[:, n * D_MODEL:(n + 1) * D_MODEL].astype(F32)
        term = gate * _dot(y, wb_ref[n])
        merged = term if merged is None else merged + term
    o_ref[...] = x_ref[...] + _dot(merged.astype(BF16), wo_ref[...])


def _merge(x, ya, yb, mq, gates, mk, mv, wb, wo, seq):
    t = x.shape[0]
    tm = MERGE_TOKENS
    n_s = seq // tm

    def row(width):
        return pl.BlockSpec((tm, width), lambda i: (i, 0))

    memspec = pl.BlockSpec((N_MEM, MEM_WIDTH), lambda i: (i // n_s, 0))
    return pl.pallas_call(
        _merge_body,
        grid=(t // tm,),
        in_specs=[row(D_MODEL), row(D_MODEL), row(D_MODEL), row(MEM_WIDTH), row(N_BRANCH * D_MODEL),
                  memspec, memspec, _const_spec(wb.shape), _const_spec(wo.shape)],
        out_specs=row(D_MODEL),
        out_shape=jax.ShapeDtypeStruct((t, D_MODEL), F32),
        compiler_params=_params("parallel"),
        name="merge",
    )(x, ya, yb, mq, gates, mk, mv, wb, wo)


def _rotary_tables(seq):
    pos = jnp.arange(seq, dtype=F32)
    inv_freq = 1.0 / (ROPE_THETA ** (jnp.arange(0, ROPE_DIM, 2, dtype=F32) / ROPE_DIM))
    ang = pos[:, None] * inv_freq[None, :]
    cos, sin = jnp.cos(ang), jnp.sin(ang)
    zero = jnp.zeros_like(cos)

    def lay(lead, a, b, width):
        parts = ([lead] if lead is not None else []) + [a, b]
        used = sum(p.shape[1] for p in parts)
        return jnp.concatenate(parts + [jnp.zeros((seq, width - used), F32)], axis=1)

    scale = (NOPE_DIM + ROPE_DIM) ** -0.5
    one = jnp.ones((seq, NOPE_DIM), F32)
    lead0 = jnp.zeros((seq, NOPE_DIM), F32)
    q_tabs = (lay(one, cos, cos, QK_PAD) * scale, lay(lead0, -sin, zero, QK_PAD) * scale,
              lay(lead0, zero, sin, QK_PAD) * scale)
    k_tabs = (lay(None, cos, cos, 128), lay(None, -sin, zero, 128), lay(None, zero, sin, 128))
    return q_tabs + k_tabs


def _pad_heads(w, widths, pad_to):
    fan_in = w.shape[0]
    per = sum(widths)
    w = w.reshape(fan_in, -1, per)[:, :, :widths[0]]
    w = jnp.pad(w, ((0, 0), (0, 0), (0, pad_to - widths[0])))
    return w.reshape(fan_in, -1)


def kernel(x, mem, ffn1_norm, ffn1_w_in, ffn1_w_down, mix_norm, w_in, b_gate, q_norm, w_uq, kv_norm, w_ukv,
           conv_w, conv_b, w_rg_a, b_rg_a, w_rg_i, b_rg_i, lru_lambda, mem_norm, w_mem_kv, w_branch, w_out,
           ffn2_norm, ffn2_w_in, ffn2_w_down, final_norm):
    bsz, seq, _ = x.shape
    depth = ffn1_norm.shape[0]
    xt = x.reshape(bsz * seq, D_MODEL)
    memt = mem.reshape(bsz * N_MEM, D_MODEL)
    tables = _rotary_tables(seq)
    fn = final_norm.reshape(1, D_MODEL)
    bf = lambda a: a.astype(BF16)

    for l in range(depth):
        last = l == depth - 1
        xt = _ffn(xt, ffn1_norm[l].reshape(1, -1), bf(ffn1_w_in[l][:, :FFN_HIDDEN]), bf(ffn1_w_in[l][:, FFN_HIDDEN:]),
                  bf(ffn1_w_down[l]), fn, final=False)

        w = w_in[l]
        c0 = Q_LORA + KV_LORA + ROPE_DIM
        wlat = bf(jnp.pad(w[:, :c0], ((0, 0), (0, LAT_WIDTH - c0))))
        wx = bf(w[:, c0:c0 + RNN_WIDTH])
        wg = bf(w[:, c0 + RNN_WIDTH:c0 + 2 * RNN_WIDTH])
        c1 = c0 + 2 * RNN_WIDTH
        wmq = bf(w[:, c1:c1 + MEM_WIDTH])
        wgate = bf(w[:, c1 + MEM_WIDTH:])
        wuq = bf(_pad_heads(w_uq[l], (NOPE_DIM + ROPE_DIM,), QK_PAD))
        wuk = bf(_pad_heads(w_ukv[l], (NOPE_DIM, V_DIM), NOPE_DIM))
        wuv = bf(w_ukv[l].reshape(KV_LORA, MLA_HEADS, NOPE_DIM + V_DIM)[:, :, NOPE_DIM:].reshape(KV_LORA, -1))

        q, kn, kr, v, zx, gz, mq, gates = _inproj(
            xt, seq, mix_norm[l].reshape(1, -1), wlat, wx, wg, wmq, wgate, b_gate[l].reshape(1, -1),
            q_norm[l].reshape(1, -1), wuq, kv_norm[l].reshape(1, -1), wuk, wuv, tables)

        ya = _mla_attention(q, kn, kr, v, seq)

        wai = bf(jnp.concatenate([w_rg_a[l], w_rg_i[l]], axis=-1))
        bai = jnp.concatenate([b_rg_a[l], b_rg_i[l]], axis=-1).reshape(RNN_BLOCKS, 1, 2 * RNN_BLOCK_DIM)
        yb = _rglru(zx, gz, conv_w[l].reshape(CONV_WIDTH, RNN_WIDTH), conv_b[l].reshape(1, -1), wai, bai,
                    lru_lambda[l].reshape(1, -1), seq)

        mk, mv = _memkv(memt, mem_norm[l].reshape(1, -1), bf(w_mem_kv[l][:, :MEM_WIDTH]),
                        bf(w_mem_kv[l][:, MEM_WIDTH:]))
        xt = _merge(xt, ya, yb, mq, gates, mk, mv, bf(w_branch[l]), bf(w_out[l]), seq)

        xt = _ffn(xt, ffn2_norm[l].reshape(1, -1), bf(ffn2_w_in[l][:, :FFN_HIDDEN]), bf(ffn2_w_in[l][:, FFN_HIDDEN:]),
                  bf(ffn2_w_down[l]), fn, final=last)
    return xt.reshape(bsz, seq, D_MODEL)
```

```python
import functools

import jax
import jax.numpy as jnp
import numpy as np
from jax import lax
from jax.experimental import pallas as pl
from jax.experimental.pallas import tpu as pltpu

F32 = jnp.float32
BF16 = jnp.bfloat16

D_MODEL = 1024
CHUNK = 64
N_MEM = 256
RMS_EPS = 1e-6
FFN_HIDDEN = 2816
MLA_HEADS = 8
Q_LORA = 384
KV_LORA = 256
NOPE_DIM = 128
ROPE_DIM = 64
V_DIM = 128
ROPE_THETA = 10000.0
RNN_WIDTH = 1024
RNN_BLOCKS = 8
RNN_BLOCK_DIM = RNN_WIDTH // RNN_BLOCKS
CONV_WIDTH = 4
LRU_C = 8.0
MEM_HEADS = 4
MEM_HEAD_DIM = 256
MEM_WIDTH = MEM_HEADS * MEM_HEAD_DIM
N_BRANCH = 3

LOG2E = float(np.log2(np.e))
Q_SCALE = (NOPE_DIM + ROPE_DIM) ** -0.5 * LOG2E
HALF_ROPE = ROPE_DIM // 2
LANES = 128
QK_PAD = 256
LAT_WIDTH = 768
HALO = 8

VMEM_LIMIT_BYTES = 56 * 1024 * 1024

FFN_TOKENS = 512
FFN_CHUNKS = ((0, 1024), (1024, 2048), (2048, FFN_HIDDEN))
MIX_TOKENS = 256
MIX_GROUPS = 4
MIX_GROUP_WIDTH = RNN_WIDTH // MIX_GROUPS
SCAN_SEG = 32
SCAN_NSEG = MIX_TOKENS // SCAN_SEG
SCAN_PITCH = 40
ATTN_Q = 256
ATTN_SLAB = 64
MERGE_TOKENS = 256


def _dot(a, b):
    return jnp.dot(a, b, preferred_element_type=F32)


def _dot_nt(a, b):
    return lax.dot_general(a, b, (((1,), (1,)), ((), ())), preferred_element_type=F32)


def _rms(x, g):
    return x * lax.rsqrt(jnp.mean(x * x, axis=-1, keepdims=True) + RMS_EPS) * g


def _const_spec(shape):
    nd = len(shape)
    return pl.BlockSpec(shape, lambda *_: (0,) * nd, pipeline_mode=pl.Buffered(1))


def _params(*sem):
    return pltpu.CompilerParams(dimension_semantics=sem, vmem_limit_bytes=VMEM_LIMIT_BYTES)


def _ffn_body(x_ref, g_ref, wg_ref, wu_ref, wd_ref, fn_ref, o_ref, *, final):
    x = x_ref[...]
    h = _rms(x, g_ref[...]).astype(BF16)
    acc = None
    for lo, hi in FFN_CHUNKS:
        gate = _dot(h, wg_ref[:, lo:hi])
        up = _dot(h, wu_ref[:, lo:hi])
        a = (gate * jax.nn.sigmoid(gate) * up).astype(BF16)
        part = _dot(a, wd_ref[lo:hi, :])
        acc = part if acc is None else acc + part
    y = x + 0.5 * acc
    if final:
        y = _rms(y, fn_ref[...])
    o_ref[...] = y


def _ffn(x, g, wg, wu, wd, fn, *, final):
    t = x.shape[0]
    tm = FFN_TOKENS
    row = pl.BlockSpec((tm, D_MODEL), lambda i: (i, 0))
    return pl.pallas_call(
        functools.partial(_ffn_body, final=final),
        grid=(t // tm,),
        in_specs=[row, _const_spec((1, D_MODEL)), _const_spec(wg.shape), _const_spec(wu.shape),
                  _const_spec(wd.shape), _const_spec((1, D_MODEL))],
        out_specs=row,
        out_shape=jax.ShapeDtypeStruct((t, D_MODEL), F32),
        compiler_params=_params("parallel"),
        name="ffn_final" if final else "ffn",
    )(x, g, wg, wu, wd, fn)


def _gelu_tanh(x):
    return 0.5 * x * (1.0 + jnp.tanh(np.sqrt(2.0 / np.pi) * (x + 0.044715 * (x * x * x))))


def _rot(x, c, s_lo, s_hi):
    w = x.shape[-1]
    return x * c + pltpu.roll(x, w - HALF_ROPE, 1) * s_lo + pltpu.roll(x, HALF_ROPE, 1) * s_hi


def _rglru_coeffs(xc, n0, wai_ref, bai_ref, neg_c_log2, pos_c, a_s, u_s):
    xcb = xc.astype(BF16)
    for k in range(xc.shape[1] // RNN_BLOCK_DIM):
        n = n0 + k
        loc = slice(k * RNN_BLOCK_DIM, (k + 1) * RNN_BLOCK_DIM)
        blk = slice(n * RNN_BLOCK_DIM, (n + 1) * RNN_BLOCK_DIM)
        g = _dot(xcb[:, loc], wai_ref[n]) + bai_ref[n]
        r = jax.nn.sigmoid(g[:, :RNN_BLOCK_DIM])
        ig = jax.nn.sigmoid(g[:, RNN_BLOCK_DIM:])
        a = jnp.exp2(r * neg_c_log2[:, blk])
        y = jnp.tanh(r * pos_c[:, blk]) * (1.0 + a * a)
        root = jnp.where(y == 0.0, 0.0, y * lax.rsqrt(y))
        u = root * (ig * xc[:, loc])
        for s in range(SCAN_NSEG):
            src = slice(s * SCAN_SEG, (s + 1) * SCAN_SEG)
            dst = slice(s * SCAN_PITCH, s * SCAN_PITCH + SCAN_SEG)
            a_s[n, dst, :] = a[src, :]
            u_s[n, dst, :] = u[src, :]


def _segmented_scan(a_s, u_s, n, h_in):
    h_loc = prod = None
    for j in range(SCAN_SEG):
        rows = pl.ds(j, SCAN_NSEG, stride=SCAN_PITCH)
        aj, uj = a_s[n, rows, :], u_s[n, rows, :]
        h_loc = uj if j == 0 else aj * h_loc + uj
        prod = aj if j == 0 else aj * prod
        u_s[n, rows, :] = h_loc
        a_s[n, rows, :] = prod
    seg_in = []
    for s in range(SCAN_NSEG):
        seg_in.append(h_in)
        last = s * SCAN_PITCH + SCAN_SEG - 1
        h_in = a_s[n, last:last + 1, :] * h_in + u_s[n, last:last + 1, :]
    return seg_in, h_in


def _mix_in_body(x_ref, gmix_ref, wlat_ref, wx_ref, wg_ref, wmq_ref, wgate_ref, bgate_ref,
                 qnorm_ref, wuq_ref, kvnorm_ref, wuk_ref, wuvt_ref,
                 cw_ref, cb_ref, wai_ref, bai_ref, lam_ref,
                 cq_ref, sq_lo_ref, sq_hi_ref, ck_ref, sk_lo_ref, sk_hi_ref,
                 q_out, k_out, vt_out, yb_out, mq_out, gate_out,
                 xbuf, hstate, a_s, u_s):
    ts = MIX_TOKENS

    @pl.when(pl.program_id(1) == 0)
    def _():
        xbuf[0:HALO, :] = jnp.zeros((HALO, RNN_WIDTH), F32)
        hstate[...] = jnp.zeros((1, RNN_WIDTH), F32)

    h = _rms(x_ref[...], gmix_ref[...]).astype(BF16)
    lat = _dot(h, wlat_ref[...])
    hq = _rms(lat[:, :Q_LORA], qnorm_ref[...]).astype(BF16)
    hkv = _rms(lat[:, Q_LORA:Q_LORA + KV_LORA], kvnorm_ref[...]).astype(BF16)
    kr = _rot(lat[:, Q_LORA + KV_LORA:], ck_ref[...], sk_lo_ref[...], sk_hi_ref[...]).astype(BF16)
    cq, sq_lo, sq_hi = cq_ref[...], sq_lo_ref[...], sq_hi_ref[...]

    lam = lam_ref[...]
    sp = jnp.maximum(-lam, 0.0) + jnp.log1p(jnp.exp(-jnp.abs(lam)))
    neg_c_log2 = (-LRU_C * LOG2E) * sp
    pos_c = LRU_C * sp

    for grp in range(MIX_GROUPS):
        cols = slice(grp * MIX_GROUP_WIDTH, (grp + 1) * MIX_GROUP_WIDTH)

        xbuf[HALO:HALO + ts, cols] = _dot(h, wx_ref[:, cols])
        xc = cb_ref[:, cols] + cw_ref[CONV_WIDTH - 1:CONV_WIDTH, cols] * xbuf[HALO:HALO + ts, cols]
        for d in range(1, CONV_WIDTH):
            xc = xc + cw_ref[CONV_WIDTH - 1 - d:CONV_WIDTH - d, cols] * xbuf[HALO - d:HALO - d + ts, cols]
        xbuf[0:HALO, cols] = xbuf[ts:ts + HALO, cols]
        n0 = grp * (MIX_GROUP_WIDTH // RNN_BLOCK_DIM)
        _rglru_coeffs(xc, n0, wai_ref, bai_ref, neg_c_log2, pos_c, a_s, u_s)
        gz = _gelu_tanh(_dot(h, wg_ref[:, cols]))
        for k in range(MIX_GROUP_WIDTH // RNN_BLOCK_DIM):
            n = n0 + k
            blk = slice(n * RNN_BLOCK_DIM, (n + 1) * RNN_BLOCK_DIM)
            seg_in, h_last = _segmented_scan(a_s, u_s, n, hstate[:, blk])
            hstate[:, blk] = h_last
            for s, h_seg in enumerate(seg_in):
                rows = slice(s * SCAN_SEG, (s + 1) * SCAN_SEG)
                pad = slice(s * SCAN_PITCH, s * SCAN_PITCH + SCAN_SEG)
                h_true = u_s[n, pad, :] + a_s[n, pad, :] * h_seg
                yb_out[rows, blk] = (h_true * gz[rows, k * RNN_BLOCK_DIM:(k + 1) * RNN_BLOCK_DIM]).astype(BF16)

        heads = MIX_GROUP_WIDTH // NOPE_DIM
        q = _dot(hq, wuq_ref[:, grp * heads * QK_PAD:(grp + 1) * heads * QK_PAD])
        kn = _dot(hkv, wuk_ref[:, cols]).astype(BF16)
        for k in range(heads):
            base = (grp * heads + k) * QK_PAD
            q_out[:, base:base + NOPE_DIM] = (q[:, k * QK_PAD:k * QK_PAD + NOPE_DIM] * Q_SCALE).astype(BF16)
            q_out[:, base + NOPE_DIM:base + QK_PAD] = _rot(
                q[:, k * QK_PAD + NOPE_DIM:(k + 1) * QK_PAD], cq, sq_lo, sq_hi).astype(BF16)
            k_out[:, base:base + NOPE_DIM] = kn[:, k * NOPE_DIM:(k + 1) * NOPE_DIM]
            k_out[:, base + NOPE_DIM:base + QK_PAD] = kr
        vt_out[cols, :] = _dot_nt(wuvt_ref[cols, :], hkv).astype(BF16)

        mq_out[:, cols] = (_dot(h, wmq_ref[:, cols]) * (MEM_HEAD_DIM ** -0.5)).astype(BF16)
        gcols = slice(grp * N_BRANCH * MIX_GROUP_WIDTH, (grp + 1) * N_BRANCH * MIX_GROUP_WIDTH)
        gate_out[:, gcols] = jax.nn.sigmoid(_dot(h, wgate_ref[:, gcols]) + bgate_ref[:, gcols]).astype(BF16)


def _mix_in(x, seq, consts, tables):
    t = x.shape[0]
    tm = MIX_TOKENS
    n_s = seq // tm
    bsz = t // seq

    def row(width):
        return pl.BlockSpec((tm, width), lambda b, i: (b * n_s + i, 0))

    def tab(width):
        return pl.BlockSpec((tm, width), lambda b, i: (i, 0))

    row_outs = [(MLA_HEADS * QK_PAD, BF16), (MLA_HEADS * QK_PAD, BF16)]
    tail_outs = [(RNN_WIDTH, BF16), (MEM_WIDTH, BF16), (N_BRANCH * D_MODEL, BF16)]
    vt_spec = pl.BlockSpec((MLA_HEADS * V_DIM, tm), lambda b, i: (b, i))
    vt_shape = jax.ShapeDtypeStruct((bsz * MLA_HEADS * V_DIM, seq), BF16)
    return pl.pallas_call(
        _mix_in_body,
        grid=(bsz, n_s),
        in_specs=[row(D_MODEL)] + [_const_spec(c.shape) for c in consts] + [tab(LANES)] * 6,
        out_specs=[row(w) for w, _ in row_outs] + [vt_spec] + [row(w) for w, _ in tail_outs],
        out_shape=[jax.ShapeDtypeStruct((t, w), dt) for w, dt in row_outs] + [vt_shape]
                  + [jax.ShapeDtypeStruct((t, w), dt) for w, dt in tail_outs],
        scratch_shapes=[pltpu.VMEM((HALO + tm, RNN_WIDTH), F32), pltpu.VMEM((1, RNN_WIDTH), F32),
                        pltpu.VMEM((RNN_BLOCKS, SCAN_NSEG * SCAN_PITCH, RNN_BLOCK_DIM), F32),
                        pltpu.VMEM((RNN_BLOCKS, SCAN_NSEG * SCAN_PITCH, RNN_BLOCK_DIM), F32)],
        compiler_params=_params("parallel", "arbitrary"),
        name="mix_in",
    )(x, *consts, *tables)


def _col_reduce(op, x):
    n, w = x.shape
    return op(op(x.reshape(n // ATTN_SLAB, ATTN_SLAB, w), axis=0), axis=0, keepdims=True)


def _attn_body(q_ref, k_ref, vt_ref, o_ref, s_buf, p_buf):
    seq = q_ref.shape[0]
    tq = ATTN_Q
    key_chunk = lax.broadcasted_iota(jnp.int32, (tq, tq), 0) // CHUNK
    qry_chunk = lax.broadcasted_iota(jnp.int32, (tq, tq), 1) // CHUNK
    diag_mask = key_chunk <= qry_chunk

    def scores(i):
        lo, hi = i * tq, (i + 1) * tq
        q = q_ref[lo:hi, :]
        s_buf[i % 2, lo:hi, :] = jnp.where(diag_mask, _dot_nt(k_ref[lo:hi, :], q), -jnp.inf)
        if i > 0:
            s_buf[i % 2, 0:lo, :] = _dot_nt(k_ref[0:lo, :], q)

    def softmax_pv(i):
        lo, hi = i * tq, (i + 1) * tq
        slot = i % 2
        m = _col_reduce(jnp.max, s_buf[slot, 0:hi, :])
        pt = jnp.exp2(s_buf[slot, 0:hi, :] - m)
        l = _col_reduce(jnp.sum, pt)
        p_buf[slot, 0:hi, :] = pt.astype(BF16)
        acc = _dot(vt_ref[:, 0:hi], p_buf[slot, 0:hi, :])
        o_ref[lo:hi, :] = (acc / l).T.astype(BF16)

    order = list(reversed(range(seq // tq)))
    scores(order[0])
    for pos, i in enumerate(order):
        if pos + 1 < len(order):
            scores(order[pos + 1])
        softmax_pv(i)


def _mla_attention(q, k, vt, seq):
    t = q.shape[0]
    bsz = t // seq
    return pl.pallas_call(
        _attn_body,
        grid=(bsz, MLA_HEADS),
        in_specs=[pl.BlockSpec((seq, QK_PAD), lambda b, h: (b, h)),
                  pl.BlockSpec((seq, QK_PAD), lambda b, h: (b, h)),
                  pl.BlockSpec((V_DIM, seq), lambda b, h: (b * MLA_HEADS + h, 0))],
        out_specs=pl.BlockSpec((seq, V_DIM), lambda b, h: (b, h)),
        out_shape=jax.ShapeDtypeStruct((t, MLA_HEADS * V_DIM), BF16),
        scratch_shapes=[pltpu.VMEM((2, seq, ATTN_Q), F32), pltpu.VMEM((2, seq, ATTN_Q), BF16)],
        compiler_params=_params("parallel", "parallel"),
        name="mla_attn",
    )(q, k, vt)


def _memkv_body(m_ref, g_ref, wk_ref, wv_ref, k_out, v_out):
    h = _rms(m_ref[...], g_ref[...]).astype(BF16)
    k_out[...] = _dot(h, wk_ref[...]).astype(BF16)
    v_out[...] = _dot(h, wv_ref[...]).astype(BF16)


def _memkv(mem, g, wk, wv):
    t = mem.shape[0]
    row = pl.BlockSpec((N_MEM, D_MODEL), lambda b: (b, 0))
    out = pl.BlockSpec((N_MEM, MEM_WIDTH), lambda b: (b, 0))
    return pl.pallas_call(
        _memkv_body,
        grid=(t // N_MEM,),
        in_specs=[row, _const_spec(g.shape), _const_spec(wk.shape), _const_spec(wv.shape)],
        out_specs=[out, out],
        out_shape=[jax.ShapeDtypeStruct((t, MEM_WIDTH), BF16)] * 2,
        compiler_params=_params("parallel"),
        name="memkv",
    )(mem, g, wk, wv)


def _merge_body(x_ref, ya_ref, yb_ref, mq_ref, gate_ref, mk_ref, mv_ref, wb_ref, wo_ref, o_ref):
    ycs = []
    for hd in range(MEM_HEADS):
        sl = slice(hd * MEM_HEAD_DIM, (hd + 1) * MEM_HEAD_DIM)
        s = _dot_nt(mq_ref[:, sl], mk_ref[:, sl])
        p = jnp.exp(s - jnp.max(s, axis=-1, keepdims=True))
        l = jnp.sum(p, axis=-1, keepdims=True)
        ycs.append((_dot(p.astype(BF16), mv_ref[:, sl]) / l).astype(BF16))
    yc = jnp.concatenate(ycs, axis=1)

    merged = None
    for n, y in enumerate((ya_ref[...], yb_ref[...], yc)):
        gate = gate_ref[:, n * D_MODEL:(n + 1) * D_MODEL].astype(F32)
        term = gate * _dot(y, wb_ref[n])
        merged = term if merged is None else merged + term
    o_ref[...] = x_ref[...] + _dot(merged.astype(BF16), wo_ref[...])


def _merge(x, ya, yb, mq, gates, mk, mv, wb, wo, seq):
    t = x.shape[0]
    tm = MERGE_TOKENS
    n_s = seq // tm

    def row(width):
        return pl.BlockSpec((tm, width), lambda i: (i, 0))

    memspec = pl.BlockSpec((N_MEM, MEM_WIDTH), lambda i: (i // n_s, 0))
    return pl.pallas_call(
        _merge_body,
        grid=(t // tm,),
        in_specs=[row(D_MODEL), row(D_MODEL), row(D_MODEL), row(MEM_WIDTH), row(N_BRANCH * D_MODEL),
                  memspec, memspec, _const_spec(wb.shape), _const_spec(wo.shape)],
        out_specs=row(D_MODEL),
        out_shape=jax.ShapeDtypeStruct((t, D_MODEL), F32),
        compiler_params=_params("parallel"),
        name="merge",
    )(x, ya, yb, mq, gates, mk, mv, wb, wo)


def _rotary_tables(seq):
    pos = jnp.arange(seq, dtype=F32)
    inv_freq = 1.0 / (ROPE_THETA ** (jnp.arange(0, ROPE_DIM, 2, dtype=F32) / ROPE_DIM))
    ang = pos[:, None] * inv_freq[None, :]
    cos, sin = jnp.cos(ang), jnp.sin(ang)
    zero = jnp.zeros_like(cos)

    def lay(a, b):
        return jnp.concatenate([a, b, jnp.zeros((seq, LANES - 2 * HALF_ROPE), F32)], axis=1)

    k_tabs = (lay(cos, cos), lay(-sin, zero), lay(zero, sin))
    q_tabs = tuple(t * Q_SCALE for t in k_tabs)
    return q_tabs + k_tabs


def _pad_heads(w, keep, per_head, pad_to):
    fan_in = w.shape[0]
    w = w.reshape(fan_in, -1, per_head)[:, :, :keep]
    w = jnp.pad(w, ((0, 0), (0, 0), (0, pad_to - keep)))
    return w.reshape(fan_in, -1)


def kernel(x, mem, ffn1_norm, ffn1_w_in, ffn1_w_down, mix_norm, w_in, b_gate, q_norm, w_uq, kv_norm, w_ukv,
           conv_w, conv_b, w_rg_a, b_rg_a, w_rg_i, b_rg_i, lru_lambda, mem_norm, w_mem_kv, w_branch, w_out,
           ffn2_norm, ffn2_w_in, ffn2_w_down, final_norm):
    bsz, seq, _ = x.shape
    depth = ffn1_norm.shape[0]
    xt = x.reshape(bsz * seq, D_MODEL)
    memt = mem.reshape(bsz * N_MEM, D_MODEL)
    tables = _rotary_tables(seq)
    fn = final_norm.reshape(1, D_MODEL)
    bf = lambda a: a.astype(BF16)

    for l in range(depth):
        last = l == depth - 1
        xt = _ffn(xt, ffn1_norm[l].reshape(1, -1), bf(ffn1_w_in[l][:, :FFN_HIDDEN]), bf(ffn1_w_in[l][:, FFN_HIDDEN:]),
                  bf(ffn1_w_down[l]), fn, final=False)

        w = w_in[l]
        c0 = Q_LORA + KV_LORA + ROPE_DIM
        c1 = c0 + 2 * RNN_WIDTH
        wlat = bf(jnp.pad(w[:, :c0], ((0, 0), (0, LAT_WIDTH - c0))))
        wx = bf(w[:, c0:c0 + RNN_WIDTH])
        wg = bf(w[:, c0 + RNN_WIDTH:c1])
        wmq = bf(w[:, c1:c1 + MEM_WIDTH])
        wgate = bf(w[:, c1 + MEM_WIDTH:])
        wuq = bf(_pad_heads(w_uq[l], NOPE_DIM + ROPE_DIM, NOPE_DIM + ROPE_DIM, QK_PAD))
        wukv = w_ukv[l].reshape(KV_LORA, MLA_HEADS, NOPE_DIM + V_DIM)
        wuk = bf(wukv[:, :, :NOPE_DIM].reshape(KV_LORA, -1))
        wuvt = bf(wukv[:, :, NOPE_DIM:].reshape(KV_LORA, -1).T)
        wai = bf(jnp.concatenate([w_rg_a[l], w_rg_i[l]], axis=-1))
        bai = jnp.concatenate([b_rg_a[l], b_rg_i[l]], axis=-1).reshape(RNN_BLOCKS, 1, 2 * RNN_BLOCK_DIM)
        consts = [mix_norm[l].reshape(1, -1), wlat, wx, wg, wmq, wgate, b_gate[l].reshape(1, -1),
                  q_norm[l].reshape(1, -1), wuq, kv_norm[l].reshape(1, -1), wuk, wuvt,
                  conv_w[l].reshape(CONV_WIDTH, RNN_WIDTH), conv_b[l].reshape(1, -1), wai, bai,
                  lru_lambda[l].reshape(1, -1)]
        q, k, vt, yb, mq, gates = _mix_in(xt, seq, consts, tables)

        ya = _mla_attention(q, k, vt, seq)
        mk, mv = _memkv(memt, mem_norm[l].reshape(1, -1), bf(w_mem_kv[l][:, :MEM_WIDTH]),
                        bf(w_mem_kv[l][:, MEM_WIDTH:]))
        xt = _merge(xt, ya, yb, mq, gates, mk, mv, bf(w_branch[l]), bf(w_out[l]), seq)

        xt = _ffn(xt, ffn2_norm[l].reshape(1, -1), bf(ffn2_w_in[l][:, :FFN_HIDDEN]), bf(ffn2_w_in[l][:, FFN_HIDDEN:]),
                  bf(ffn2_w_down[l]), fn, final=last)
    return xt.reshape(bsz, seq, D_MODEL)
```

```python
import functools

import jax
import jax.numpy as jnp
import numpy as np
from jax import lax
from jax.experimental import pallas as pl
from jax.experimental.pallas import tpu as pltpu

F32 = jnp.float32
BF16 = jnp.bfloat16

D_MODEL = 1024
CHUNK = 64
N_MEM = 256
RMS_EPS = 1e-6
FFN_HIDDEN = 2816
MLA_HEADS = 8
Q_LORA = 384
KV_LORA = 256
NOPE_DIM = 128
ROPE_DIM = 64
V_DIM = 128
ROPE_THETA = 10000.0
RNN_WIDTH = 1024
RNN_BLOCKS = 8
RNN_BLOCK_DIM = RNN_WIDTH // RNN_BLOCKS
CONV_WIDTH = 4
LRU_C = 8.0
MEM_HEADS = 4
MEM_HEAD_DIM = 256
MEM_WIDTH = MEM_HEADS * MEM_HEAD_DIM
N_BRANCH = 3

LOG2E = float(np.log2(np.e))
Q_SCALE = (NOPE_DIM + ROPE_DIM) ** -0.5 * LOG2E
MQ_SCALE = MEM_HEAD_DIM ** -0.5 * LOG2E
HALF_ROPE = ROPE_DIM // 2
LANES = 128
QK_PAD = 256
LAT_WIDTH = 768
HALO = 8

VMEM_LIMIT_BYTES = 56 * 1024 * 1024

FFN_TOKENS = 1024
FFN_PARTS = 4
FFN_CHUNKS = ((0, 1024), (1024, 2048), (2048, FFN_HIDDEN))
MIX_TOKENS = 512
MIX_GROUPS = 4
MIX_GROUP_WIDTH = RNN_WIDTH // MIX_GROUPS
SCAN_NSEG = 8
SCAN_SEG = MIX_TOKENS // SCAN_NSEG
SCAN_PITCH = SCAN_SEG + 8
ATTN_Q = 256
ATTN_SLAB = 64
MERGE_TOKENS = 512


def _dot(a, b):
    return jnp.dot(a, b, preferred_element_type=F32)


def _dot_nt(a, b):
    return lax.dot_general(a, b, (((1,), (1,)), ((), ())), preferred_element_type=F32)


def _rms(x, g):
    return x * lax.rsqrt(jnp.mean(x * x, axis=-1, keepdims=True) + RMS_EPS) * g


def _const_spec(shape):
    nd = len(shape)
    return pl.BlockSpec(shape, lambda *_: (0,) * nd, pipeline_mode=pl.Buffered(1))


def _params(*sem):
    return pltpu.CompilerParams(dimension_semantics=sem, vmem_limit_bytes=VMEM_LIMIT_BYTES)


def _ffn_body(x_ref, g_ref, wg_ref, wu_ref, wd_ref, fn_ref, o_ref, *, final):
    part_rows = x_ref.shape[0] // FFN_PARTS
    for r in range(FFN_PARTS):
        rows = slice(r * part_rows, (r + 1) * part_rows)
        x = x_ref[rows, :]
        h = _rms(x, g_ref[...]).astype(BF16)
        acc = None
        for lo, hi in FFN_CHUNKS:
            gate = _dot(h, wg_ref[:, lo:hi])
            up = _dot(h, wu_ref[:, lo:hi])
            a = (gate * jax.nn.sigmoid(gate) * up).astype(BF16)
            part = _dot(a, wd_ref[lo:hi, :])
            acc = part if acc is None else acc + part
        y = x + 0.5 * acc
        if final:
            y = _rms(y, fn_ref[...])
        o_ref[rows, :] = y


def _ffn(x, g, wg, wu, wd, fn, *, final):
    t = x.shape[0]
    tm = FFN_TOKENS
    row = pl.BlockSpec((tm, D_MODEL), lambda i: (i, 0))
    return pl.pallas_call(
        functools.partial(_ffn_body, final=final),
        grid=(t // tm,),
        in_specs=[row, _const_spec((1, D_MODEL)), _const_spec(wg.shape), _const_spec(wu.shape),
                  _const_spec(wd.shape), _const_spec((1, D_MODEL))],
        out_specs=row,
        out_shape=jax.ShapeDtypeStruct((t, D_MODEL), F32),
        compiler_params=_params("parallel"),
        name="ffn_final" if final else "ffn",
    )(x, g, wg, wu, wd, fn)


def _gelu_tanh(x):
    return 0.5 * x * (1.0 + jnp.tanh(np.sqrt(2.0 / np.pi) * (x + 0.044715 * (x * x * x))))


def _rot(x, c, s_lo, s_hi):
    w = x.shape[-1]
    return x * c + pltpu.roll(x, w - HALF_ROPE, 1) * s_lo + pltpu.roll(x, HALF_ROPE, 1) * s_hi


def _rglru_coeffs(xc, gates, n0, bai_ref, neg_c_log2, pos_c, a_s, u_s):
    for k, g in enumerate(gates):
        n = n0 + k
        loc = slice(k * RNN_BLOCK_DIM, (k + 1) * RNN_BLOCK_DIM)
        blk = slice(n * RNN_BLOCK_DIM, (n + 1) * RNN_BLOCK_DIM)
        g = g + bai_ref[n]
        r = jax.nn.sigmoid(g[:, :RNN_BLOCK_DIM])
        ig = jax.nn.sigmoid(g[:, RNN_BLOCK_DIM:])
        a = jnp.exp2(r * neg_c_log2[:, blk])
        y = jnp.tanh(r * pos_c[:, blk]) * (1.0 + a * a)
        root = jnp.where(y == 0.0, 0.0, y * lax.rsqrt(y))
        u = root * (ig * xc[:, loc])
        for s in range(SCAN_NSEG):
            src = slice(s * SCAN_SEG, (s + 1) * SCAN_SEG)
            dst = slice(s * SCAN_PITCH, s * SCAN_PITCH + SCAN_SEG)
            a_s[n, dst, :] = a[src, :]
            u_s[n, dst, :] = u[src, :]


def _segmented_scan(a_s, u_s, n, h_in):
    h_loc = prod = None
    for j in range(SCAN_SEG):
        rows = pl.ds(j, SCAN_NSEG, stride=SCAN_PITCH)
        aj, uj = a_s[n, rows, :], u_s[n, rows, :]
        h_loc = uj if j == 0 else aj * h_loc + uj
        prod = aj if j == 0 else aj * prod
        u_s[n, rows, :] = h_loc
        a_s[n, rows, :] = prod
    seg_in = []
    for s in range(SCAN_NSEG):
        seg_in.append(h_in)
        last = s * SCAN_PITCH + SCAN_SEG - 1
        h_in = a_s[n, last:last + 1, :] * h_in + u_s[n, last:last + 1, :]
    return seg_in, h_in


def _mix_in_body(x_ref, gmix_ref, wlat_ref, wx_ref, wg_ref, wmq_ref, wgate_ref, bgate_ref,
                 qnorm_ref, wuq_ref, kvnorm_ref, wuk_ref, wuvt_ref,
                 cw_ref, cb_ref, wai_ref, bai_ref, lam_ref,
                 cq_ref, sq_lo_ref, sq_hi_ref, ck_ref, sk_lo_ref, sk_hi_ref,
                 q_out, k_out, vt_out, yb_out, mq_out, gate_out,
                 xbuf, hstate, a_s, u_s):
    ts = MIX_TOKENS

    @pl.when(pl.program_id(1) == 0)
    def _():
        xbuf[0:HALO, :] = jnp.zeros((HALO, RNN_WIDTH), F32)
        hstate[...] = jnp.zeros((1, RNN_WIDTH), F32)

    h = _rms(x_ref[...], gmix_ref[...]).astype(BF16)
    lat = _dot(h, wlat_ref[...])
    hq = _rms(lat[:, :Q_LORA], qnorm_ref[...]).astype(BF16)
    hkv = _rms(lat[:, Q_LORA:Q_LORA + KV_LORA], kvnorm_ref[...]).astype(BF16)
    kr = _rot(lat[:, Q_LORA + KV_LORA:], ck_ref[...], sk_lo_ref[...], sk_hi_ref[...]).astype(BF16)
    cq, sq_lo, sq_hi = cq_ref[...], sq_lo_ref[...], sq_hi_ref[...]

    lam = lam_ref[...]
    sp = jnp.maximum(-lam, 0.0) + jnp.log1p(jnp.exp(-jnp.abs(lam)))
    neg_c_log2 = (-LRU_C * LOG2E) * sp
    pos_c = LRU_C * sp

    blocks = MIX_GROUP_WIDTH // RNN_BLOCK_DIM
    heads = MIX_GROUP_WIDTH // NOPE_DIM
    for grp in range(MIX_GROUPS):
        cols = slice(grp * MIX_GROUP_WIDTH, (grp + 1) * MIX_GROUP_WIDTH)
        gcols = slice(grp * N_BRANCH * MIX_GROUP_WIDTH, (grp + 1) * N_BRANCH * MIX_GROUP_WIDTH)
        n0 = grp * blocks

        xbuf[HALO:HALO + ts, cols] = _dot(h, wx_ref[:, cols])
        q = _dot(hq, wuq_ref[:, grp * heads * QK_PAD:(grp + 1) * heads * QK_PAD])
        kn = _dot(hkv, wuk_ref[:, cols]).astype(BF16)
        vt_out[cols, :] = _dot_nt(wuvt_ref[cols, :], hkv).astype(BF16)

        xc = cb_ref[:, cols] + cw_ref[CONV_WIDTH - 1:CONV_WIDTH, cols] * xbuf[HALO:HALO + ts, cols]
        for d in range(1, CONV_WIDTH):
            xc = xc + cw_ref[CONV_WIDTH - 1 - d:CONV_WIDTH - d, cols] * xbuf[HALO - d:HALO - d + ts, cols]
        xbuf[0:HALO, cols] = xbuf[ts:ts + HALO, cols]
        xcb = xc.astype(BF16)
        gates = [_dot(xcb[:, k * RNN_BLOCK_DIM:(k + 1) * RNN_BLOCK_DIM], wai_ref[n0 + k]) for k in range(blocks)]

        mq = _dot(h, wmq_ref[:, cols])
        gz = _dot(h, wg_ref[:, cols])
        merge_gate = _dot(h, wgate_ref[:, gcols])

        _rglru_coeffs(xc, gates, n0, bai_ref, neg_c_log2, pos_c, a_s, u_s)
        gz = _gelu_tanh(gz)
        for k in range(blocks):
            n = n0 + k
            blk = slice(n * RNN_BLOCK_DIM, (n + 1) * RNN_BLOCK_DIM)
            seg_in, h_last = _segmented_scan(a_s, u_s, n, hstate[:, blk])
            hstate[:, blk] = h_last
            for s, h_seg in enumerate(seg_in):
                rows = slice(s * SCAN_SEG, (s + 1) * SCAN_SEG)
                pad = slice(s * SCAN_PITCH, s * SCAN_PITCH + SCAN_SEG)
                h_true = u_s[n, pad, :] + a_s[n, pad, :] * h_seg
                yb_out[rows, blk] = (h_true * gz[rows, k * RNN_BLOCK_DIM:(k + 1) * RNN_BLOCK_DIM]).astype(BF16)

        for k in range(heads):
            base = (grp * heads + k) * QK_PAD
            q_out[:, base:base + NOPE_DIM] = (q[:, k * QK_PAD:k * QK_PAD + NOPE_DIM] * Q_SCALE).astype(BF16)
            q_out[:, base + NOPE_DIM:base + QK_PAD] = _rot(
                q[:, k * QK_PAD + NOPE_DIM:(k + 1) * QK_PAD], cq, sq_lo, sq_hi).astype(BF16)
            k_out[:, base:base + NOPE_DIM] = kn[:, k * NOPE_DIM:(k + 1) * NOPE_DIM]
            k_out[:, base + NOPE_DIM:base + QK_PAD] = kr
        mq_out[:, cols] = (mq * MQ_SCALE).astype(BF16)
        gate_out[:, gcols] = jax.nn.sigmoid(merge_gate + bgate_ref[:, gcols]).astype(BF16)


def _mix_in(x, seq, consts, tables):
    t = x.shape[0]
    tm = MIX_TOKENS
    n_s = seq // tm
    bsz = t // seq

    def row(width):
        return pl.BlockSpec((tm, width), lambda b, i: (b * n_s + i, 0))

    def tab(width):
        return pl.BlockSpec((tm, width), lambda b, i: (i, 0))

    row_outs = [(MLA_HEADS * QK_PAD, BF16), (MLA_HEADS * QK_PAD, BF16)]
    tail_outs = [(RNN_WIDTH, BF16), (MEM_WIDTH, BF16), (N_BRANCH * D_MODEL, BF16)]
    vt_spec = pl.BlockSpec((MLA_HEADS * V_DIM, tm), lambda b, i: (b, i))
    vt_shape = jax.ShapeDtypeStruct((bsz * MLA_HEADS * V_DIM, seq), BF16)
    return pl.pallas_call(
        _mix_in_body,
        grid=(bsz, n_s),
        in_specs=[row(D_MODEL)] + [_const_spec(c.shape) for c in consts] + [tab(LANES)] * 6,
        out_specs=[row(w) for w, _ in row_outs] + [vt_spec] + [row(w) for w, _ in tail_outs],
        out_shape=[jax.ShapeDtypeStruct((t, w), dt) for w, dt in row_outs] + [vt_shape]
                  + [jax.ShapeDtypeStruct((t, w), dt) for w, dt in tail_outs],
        scratch_shapes=[pltpu.VMEM((HALO + tm, RNN_WIDTH), F32), pltpu.VMEM((1, RNN_WIDTH), F32),
                        pltpu.VMEM((RNN_BLOCKS, SCAN_NSEG * SCAN_PITCH, RNN_BLOCK_DIM), F32),
                        pltpu.VMEM((RNN_BLOCKS, SCAN_NSEG * SCAN_PITCH, RNN_BLOCK_DIM), F32)],
        compiler_params=_params("parallel", "arbitrary"),
        name="mix_in",
    )(x, *consts, *tables)


def _col_reduce(op, x):
    n, w = x.shape
    return op(op(x.reshape(n // ATTN_SLAB, ATTN_SLAB, w), axis=0), axis=0, keepdims=True)


def _attn_body(q_ref, k_ref, vt_ref, o_ref, s_buf, p_buf):
    seq = q_ref.shape[0]
    tq = ATTN_Q
    key_chunk = lax.broadcasted_iota(jnp.int32, (tq, tq), 0) // CHUNK
    qry_chunk = lax.broadcasted_iota(jnp.int32, (tq, tq), 1) // CHUNK
    diag_mask = key_chunk <= qry_chunk

    def scores(i):
        lo, hi = i * tq, (i + 1) * tq
        q = q_ref[lo:hi, :]
        s_buf[i % 2, lo:hi, :] = jnp.where(diag_mask, _dot_nt(k_ref[lo:hi, :], q), -jnp.inf)
        if i > 0:
            s_buf[i % 2, 0:lo, :] = _dot_nt(k_ref[0:lo, :], q)

    def softmax_pv(i):
        lo, hi = i * tq, (i + 1) * tq
        slot = i % 2
        m = _col_reduce(jnp.max, s_buf[slot, 0:hi, :])
        pt = jnp.exp2(s_buf[slot, 0:hi, :] - m)
        l = _col_reduce(jnp.sum, pt)
        p_buf[slot, 0:hi, :] = pt.astype(BF16)
        acc = _dot(vt_ref[:, 0:hi], p_buf[slot, 0:hi, :])
        o_ref[lo:hi, :] = (acc / l).T.astype(BF16)

    order = list(reversed(range(seq // tq)))
    scores(order[0])
    for pos, i in enumerate(order):
        if pos + 1 < len(order):
            scores(order[pos + 1])
        softmax_pv(i)


def _mla_attention(q, k, vt, seq):
    t = q.shape[0]
    bsz = t // seq
    return pl.pallas_call(
        _attn_body,
        grid=(bsz, MLA_HEADS),
        in_specs=[pl.BlockSpec((seq, QK_PAD), lambda b, h: (b, h)),
                  pl.BlockSpec((seq, QK_PAD), lambda b, h: (b, h)),
                  pl.BlockSpec((V_DIM, seq), lambda b, h: (b * MLA_HEADS + h, 0))],
        out_specs=pl.BlockSpec((seq, V_DIM), lambda b, h: (b, h)),
        out_shape=jax.ShapeDtypeStruct((t, MLA_HEADS * V_DIM), BF16),
        scratch_shapes=[pltpu.VMEM((2, seq, ATTN_Q), F32), pltpu.VMEM((2, seq, ATTN_Q), BF16)],
        compiler_params=_params("parallel", "parallel"),
        name="mla_attn",
    )(q, k, vt)


def _memkv_body(m_ref, g_ref, wk_ref, wv_ref, k_out, v_out):
    h = _rms(m_ref[...], g_ref[...]).astype(BF16)
    k_out[...] = _dot(h, wk_ref[...]).astype(BF16)
    v_out[...] = _dot(h, wv_ref[...]).astype(BF16)


def _memkv(mem, g, wk, wv):
    t = mem.shape[0]
    row = pl.BlockSpec((N_MEM, D_MODEL), lambda b: (b, 0))
    out = pl.BlockSpec((N_MEM, MEM_WIDTH), lambda b: (b, 0))
    return pl.pallas_call(
        _memkv_body,
        grid=(t // N_MEM,),
        in_specs=[row, _const_spec(g.shape), _const_spec(wk.shape), _const_spec(wv.shape)],
        out_specs=[out, out],
        out_shape=[jax.ShapeDtypeStruct((t, MEM_WIDTH), BF16)] * 2,
        compiler_params=_params("parallel"),
        name="memkv",
    )(mem, g, wk, wv)


def _merge_body(x_ref, ya_ref, yb_ref, mq_ref, gate_ref, mk_ref, mv_ref, wb_ref, wo_ref, o_ref):
    heads = [slice(hd * MEM_HEAD_DIM, (hd + 1) * MEM_HEAD_DIM) for hd in range(MEM_HEADS)]
    scores = [_dot_nt(mq_ref[:, sl], mk_ref[:, sl]) for sl in heads]
    proj_a = _dot(ya_ref[...], wb_ref[0])
    probs = []
    for s in scores:
        p = jnp.exp2(s - jnp.max(s, axis=-1, keepdims=True))
        probs.append((p.astype(BF16), jnp.sum(p, axis=-1, keepdims=True)))
    ycs = [(_dot(p, mv_ref[:, sl]) / l).astype(BF16) for (p, l), sl in zip(probs, heads)]
    proj_b = _dot(yb_ref[...], wb_ref[1])
    proj_c = _dot(jnp.concatenate(ycs, axis=1), wb_ref[2])

    merged = None
    for n, proj in enumerate((proj_a, proj_b, proj_c)):
        term = gate_ref[:, n * D_MODEL:(n + 1) * D_MODEL].astype(F32) * proj
        merged = term if merged is None else merged + term
    o_ref[...] = x_ref[...] + _dot(merged.astype(BF16), wo_ref[...])


def _merge(x, ya, yb, mq, gates, mk, mv, wb, wo, seq):
    t = x.shape[0]
    tm = MERGE_TOKENS
    n_s = seq // tm

    def row(width):
        return pl.BlockSpec((tm, width), lambda i: (i, 0))

    memspec = pl.BlockSpec((N_MEM, MEM_WIDTH), lambda i: (i // n_s, 0))
    return pl.pallas_call(
        _merge_body,
        grid=(t // tm,),
        in_specs=[row(D_MODEL), row(D_MODEL), row(D_MODEL), row(MEM_WIDTH), row(N_BRANCH * D_MODEL),
                  memspec, memspec, _const_spec(wb.shape), _const_spec(wo.shape)],
        out_specs=row(D_MODEL),
        out_shape=jax.ShapeDtypeStruct((t, D_MODEL), F32),
        compiler_params=_params("parallel"),
        name="merge",
    )(x, ya, yb, mq, gates, mk, mv, wb, wo)


def _rotary_tables(seq):
    pos = jnp.arange(seq, dtype=F32)
    inv_freq = 1.0 / (ROPE_THETA ** (jnp.arange(0, ROPE_DIM, 2, dtype=F32) / ROPE_DIM))
    ang = pos[:, None] * inv_freq[None, :]
    cos, sin = jnp.cos(ang), jnp.sin(ang)
    zero = jnp.zeros_like(cos)

    def lay(a, b):
        return jnp.concatenate([a, b, jnp.zeros((seq, LANES - 2 * HALF_ROPE), F32)], axis=1)

    k_tabs = (lay(cos, cos), lay(-sin, zero), lay(zero, sin))
    q_tabs = tuple(t * Q_SCALE for t in k_tabs)
    return q_tabs + k_tabs


def _pad_heads(w, keep, per_head, pad_to):
    fan_in = w.shape[0]
    w = w.reshape(fan_in, -1, per_head)[:, :, :keep]
    w = jnp.pad(w, ((0, 0), (0, 0), (0, pad_to - keep)))
    return w.reshape(fan_in, -1)


def kernel(x, mem, ffn1_norm, ffn1_w_in, ffn1_w_down, mix_norm, w_in, b_gate, q_norm, w_uq, kv_norm, w_ukv,
           conv_w, conv_b, w_rg_a, b_rg_a, w_rg_i, b_rg_i, lru_lambda, mem_norm, w_mem_kv, w_branch, w_out,
           ffn2_norm, ffn2_w_in, ffn2_w_down, final_norm):
    bsz, seq, _ = x.shape
    depth = ffn1_norm.shape[0]
    xt = x.reshape(bsz * seq, D_MODEL)
    memt = mem.reshape(bsz * N_MEM, D_MODEL)
    tables = _rotary_tables(seq)
    fn = final_norm.reshape(1, D_MODEL)
    bf = lambda a: a.astype(BF16)

    for l in range(depth):
        last = l == depth - 1
        xt = _ffn(xt, ffn1_norm[l].reshape(1, -1), bf(ffn1_w_in[l][:, :FFN_HIDDEN]), bf(ffn1_w_in[l][:, FFN_HIDDEN:]),
                  bf(ffn1_w_down[l]), fn, final=False)

        w = w_in[l]
        c0 = Q_LORA + KV_LORA + ROPE_DIM
        c1 = c0 + 2 * RNN_WIDTH
        wlat = bf(jnp.pad(w[:, :c0], ((0, 0), (0, LAT_WIDTH - c0))))
        wx = bf(w[:, c0:c0 + RNN_WIDTH])
        wg = bf(w[:, c0 + RNN_WIDTH:c1])
        wmq = bf(w[:, c1:c1 + MEM_WIDTH])
        wgate = bf(w[:, c1 + MEM_WIDTH:])
        wuq = bf(_pad_heads(w_uq[l], NOPE_DIM + ROPE_DIM, NOPE_DIM + ROPE_DIM, QK_PAD))
        wukv = w_ukv[l].reshape(KV_LORA, MLA_HEADS, NOPE_DIM + V_DIM)
        wuk = bf(wukv[:, :, :NOPE_DIM].reshape(KV_LORA, -1))
        wuvt = bf(wukv[:, :, NOPE_DIM:].reshape(KV_LORA, -1).T)
        wai = bf(jnp.concatenate([w_rg_a[l], w_rg_i[l]], axis=-1))
        bai = jnp.concatenate([b_rg_a[l], b_rg_i[l]], axis=-1).reshape(RNN_BLOCKS, 1, 2 * RNN_BLOCK_DIM)
        consts = [mix_norm[l].reshape(1, -1), wlat, wx, wg, wmq, wgate, b_gate[l].reshape(1, -1),
                  q_norm[l].reshape(1, -1), wuq, kv_norm[l].reshape(1, -1), wuk, wuvt,
                  conv_w[l].reshape(CONV_WIDTH, RNN_WIDTH), conv_b[l].reshape(1, -1), wai, bai,
                  lru_lambda[l].reshape(1, -1)]
        q, k, vt, yb, mq, gates = _mix_in(xt, seq, consts, tables)

        ya = _mla_attention(q, k, vt, seq)
        mk, mv = _memkv(memt, mem_norm[l].reshape(1, -1), bf(w_mem_kv[l][:, :MEM_WIDTH]),
                        bf(w_mem_kv[l][:, MEM_WIDTH:]))
        xt = _merge(xt, ya, yb, mq, gates, mk, mv, bf(w_branch[l]), bf(w_out[l]), seq)

        xt = _ffn(xt, ffn2_norm[l].reshape(1, -1), bf(ffn2_w_in[l][:, :FFN_HIDDEN]), bf(ffn2_w_in[l][:, FFN_HIDDEN:]),
                  bf(ffn2_w_down[l]), fn, final=last)
    return xt.reshape(bsz, seq, D_MODEL)
```

```python
import functools

import jax
import jax.numpy as jnp
import numpy as np
from jax import lax
from jax.experimental import pallas as pl
from jax.experimental.pallas import tpu as pltpu

F32 = jnp.float32
BF16 = jnp.bfloat16

D_MODEL = 1024
CHUNK = 64
N_MEM = 256
RMS_EPS = 1e-6
FFN_HIDDEN = 2816
MLA_HEADS = 8
Q_LORA = 384
KV_LORA = 256
NOPE_DIM = 128
ROPE_DIM = 64
V_DIM = 128
ROPE_THETA = 10000.0
RNN_WIDTH = 1024
RNN_BLOCKS = 8
RNN_BLOCK_DIM = RNN_WIDTH // RNN_BLOCKS
CONV_WIDTH = 4
LRU_C = 8.0
MEM_HEADS = 4
MEM_HEAD_DIM = 256
MEM_WIDTH = MEM_HEADS * MEM_HEAD_DIM
N_BRANCH = 3

LOG2E = float(np.log2(np.e))
Q_SCALE = (NOPE_DIM + ROPE_DIM) ** -0.5 * LOG2E
MQ_SCALE = MEM_HEAD_DIM ** -0.5 * LOG2E
HALF_ROPE = ROPE_DIM // 2
LANES = 128
QK_PAD = 256
LAT_WIDTH = 768
HALO = 8

VMEM_LIMIT_BYTES = 56 * 1024 * 1024

FFN_TOKENS = 1024
FFN_PARTS = 4
FFN_CHUNKS = ((0, 1024), (1024, 2048), (2048, FFN_HIDDEN))
MIX_TOKENS = 512
MIX_GROUPS = 4
MIX_GROUP_WIDTH = RNN_WIDTH // MIX_GROUPS
SCAN_NSEG = 8
SCAN_SEG = MIX_TOKENS // SCAN_NSEG
SCAN_PITCH = SCAN_SEG + 8
ATTN_Q = 256
ATTN_SLAB = 64
MERGE_TOKENS = 512


def _dot(a, b):
    return jnp.dot(a, b, preferred_element_type=F32)


def _dot_nt(a, b):
    return lax.dot_general(a, b, (((1,), (1,)), ((), ())), preferred_element_type=F32)


def _rms(x, g):
    return x * lax.rsqrt(jnp.mean(x * x, axis=-1, keepdims=True) + RMS_EPS) * g


def _const_spec(shape):
    nd = len(shape)
    return pl.BlockSpec(shape, lambda *_: (0,) * nd, pipeline_mode=pl.Buffered(1))


def _params(*sem):
    return pltpu.CompilerParams(dimension_semantics=sem, vmem_limit_bytes=VMEM_LIMIT_BYTES)


def _ffn_body(x_ref, g_ref, wgu_ref, wd_ref, fn_ref, o_ref, *, final):
    part_rows = x_ref.shape[0] // FFN_PARTS
    for r in range(FFN_PARTS):
        rows = slice(r * part_rows, (r + 1) * part_rows)
        x = x_ref[rows, :]
        h = _rms(x, g_ref[...]).astype(BF16)
        acc = None
        for lo, hi in FFN_CHUNKS:
            gate = _dot(h, wgu_ref[:, lo:hi])
            up = _dot(h, wgu_ref[:, FFN_HIDDEN + lo:FFN_HIDDEN + hi])
            a = (gate * jax.nn.sigmoid(gate) * up).astype(BF16)
            part = _dot(a, wd_ref[lo:hi, :])
            acc = part if acc is None else acc + part
        y = x + 0.5 * acc
        if final:
            y = _rms(y, fn_ref[...])
        o_ref[rows, :] = y


def _ffn(x, g, wgu, wd, fn, *, final):
    t = x.shape[0]
    tm = FFN_TOKENS
    row = pl.BlockSpec((tm, D_MODEL), lambda i: (i, 0))
    return pl.pallas_call(
        functools.partial(_ffn_body, final=final),
        grid=(t // tm,),
        in_specs=[row, _const_spec((1, D_MODEL)), _const_spec(wgu.shape), _const_spec(wd.shape),
                  _const_spec((1, D_MODEL))],
        out_specs=row,
        out_shape=jax.ShapeDtypeStruct((t, D_MODEL), F32),
        compiler_params=_params("parallel"),
        name="ffn_final" if final else "ffn",
    )(x, g, wgu, wd, fn)


def _gelu_tanh(x):
    return 0.5 * x * (1.0 + jnp.tanh(np.sqrt(2.0 / np.pi) * (x + 0.044715 * (x * x * x))))


def _rot(x, c, s_lo, s_hi):
    w = x.shape[-1]
    return x * c + pltpu.roll(x, w - HALF_ROPE, 1) * s_lo + pltpu.roll(x, HALF_ROPE, 1) * s_hi


def _rglru_coeffs(xc, gates, n0, bai_ref, neg_c_log2, pos_c, a_s, u_s):
    for k, g in enumerate(gates):
        n = n0 + k
        loc = slice(k * RNN_BLOCK_DIM, (k + 1) * RNN_BLOCK_DIM)
        blk = slice(n * RNN_BLOCK_DIM, (n + 1) * RNN_BLOCK_DIM)
        g = g + bai_ref[n]
        r = jax.nn.sigmoid(g[:, :RNN_BLOCK_DIM])
        ig = jax.nn.sigmoid(g[:, RNN_BLOCK_DIM:])
        a = jnp.exp2(r * neg_c_log2[:, blk])
        y = jnp.tanh(r * pos_c[:, blk]) * (1.0 + a * a)
        root = jnp.where(y == 0.0, 0.0, y * lax.rsqrt(y))
        u = root * (ig * xc[:, loc])
        for s in range(SCAN_NSEG):
            src = slice(s * SCAN_SEG, (s + 1) * SCAN_SEG)
            dst = slice(s * SCAN_PITCH, s * SCAN_PITCH + SCAN_SEG)
            a_s[n, dst, :] = a[src, :]
            u_s[n, dst, :] = u[src, :]


def _segmented_scan(a_s, u_s, n, h_in):
    h_loc = prod = None
    for j in range(SCAN_SEG):
        rows = pl.ds(j, SCAN_NSEG, stride=SCAN_PITCH)
        aj, uj = a_s[n, rows, :], u_s[n, rows, :]
        h_loc = uj if j == 0 else aj * h_loc + uj
        prod = aj if j == 0 else aj * prod
        u_s[n, rows, :] = h_loc
        a_s[n, rows, :] = prod
    seg_in = []
    for s in range(SCAN_NSEG):
        seg_in.append(h_in)
        last = s * SCAN_PITCH + SCAN_SEG - 1
        h_in = a_s[n, last:last + 1, :] * h_in + u_s[n, last:last + 1, :]
    return seg_in, h_in


def _mix_in_body(x_ref, gmix_ref, wlat_ref, wx_ref, wg_ref, wmq_ref, wgate_ref, bgate_ref,
                 qnorm_ref, wuq_ref, kvnorm_ref, wuk_ref, wuvt_ref,
                 cw_ref, cb_ref, wai_ref, bai_ref, lam_ref,
                 cq_ref, sq_lo_ref, sq_hi_ref, ck_ref, sk_lo_ref, sk_hi_ref,
                 q_out, k_out, vt_out, yb_out, mq_out, gate_out,
                 xbuf, hstate, a_s, u_s):
    ts = MIX_TOKENS

    @pl.when(pl.program_id(1) == 0)
    def _():
        xbuf[0:HALO, :] = jnp.zeros((HALO, RNN_WIDTH), F32)
        hstate[...] = jnp.zeros((1, RNN_WIDTH), F32)

    h = _rms(x_ref[...], gmix_ref[...]).astype(BF16)
    lat = _dot(h, wlat_ref[...])
    hq = _rms(lat[:, :Q_LORA], qnorm_ref[...]).astype(BF16)
    hkv = _rms(lat[:, Q_LORA:Q_LORA + KV_LORA], kvnorm_ref[...]).astype(BF16)
    kr = _rot(lat[:, Q_LORA + KV_LORA:], ck_ref[...], sk_lo_ref[...], sk_hi_ref[...]).astype(BF16)
    cq, sq_lo, sq_hi = cq_ref[...], sq_lo_ref[...], sq_hi_ref[...]

    lam = lam_ref[...]
    sp = jnp.maximum(-lam, 0.0) + jnp.log1p(jnp.exp(-jnp.abs(lam)))
    neg_c_log2 = (-LRU_C * LOG2E) * sp
    pos_c = LRU_C * sp

    blocks = MIX_GROUP_WIDTH // RNN_BLOCK_DIM
    heads = MIX_GROUP_WIDTH // NOPE_DIM
    for grp in range(MIX_GROUPS):
        cols = slice(grp * MIX_GROUP_WIDTH, (grp + 1) * MIX_GROUP_WIDTH)
        gcols = slice(grp * N_BRANCH * MIX_GROUP_WIDTH, (grp + 1) * N_BRANCH * MIX_GROUP_WIDTH)
        n0 = grp * blocks

        xbuf[HALO:HALO + ts, cols] = _dot(h, wx_ref[:, cols])
        q = _dot(hq, wuq_ref[:, grp * heads * QK_PAD:(grp + 1) * heads * QK_PAD])
        kn = _dot(hkv, wuk_ref[:, cols]).astype(BF16)
        vt_out[cols, :] = _dot_nt(wuvt_ref[cols, :], hkv).astype(BF16)

        xc = cb_ref[:, cols] + cw_ref[CONV_WIDTH - 1:CONV_WIDTH, cols] * xbuf[HALO:HALO + ts, cols]
        for d in range(1, CONV_WIDTH):
            xc = xc + cw_ref[CONV_WIDTH - 1 - d:CONV_WIDTH - d, cols] * xbuf[HALO - d:HALO - d + ts, cols]
        xbuf[0:HALO, cols] = xbuf[ts:ts + HALO, cols]
        xcb = xc.astype(BF16)
        gates = [_dot(xcb[:, k * RNN_BLOCK_DIM:(k + 1) * RNN_BLOCK_DIM], wai_ref[n0 + k]) for k in range(blocks)]

        mq = _dot(h, wmq_ref[:, cols])
        gz = _dot(h, wg_ref[:, cols])
        merge_gate = _dot(h, wgate_ref[:, gcols])

        _rglru_coeffs(xc, gates, n0, bai_ref, neg_c_log2, pos_c, a_s, u_s)
        gz = _gelu_tanh(gz)
        for k in range(blocks):
            n = n0 + k
            blk = slice(n * RNN_BLOCK_DIM, (n + 1) * RNN_BLOCK_DIM)
            seg_in, h_last = _segmented_scan(a_s, u_s, n, hstate[:, blk])
            hstate[:, blk] = h_last
            for s, h_seg in enumerate(seg_in):
                rows = slice(s * SCAN_SEG, (s + 1) * SCAN_SEG)
                pad = slice(s * SCAN_PITCH, s * SCAN_PITCH + SCAN_SEG)
                h_true = u_s[n, pad, :] + a_s[n, pad, :] * h_seg
                yb_out[rows, blk] = (h_true * gz[rows, k * RNN_BLOCK_DIM:(k + 1) * RNN_BLOCK_DIM]).astype(BF16)

        for k in range(heads):
            base = (grp * heads + k) * QK_PAD
            q_out[:, base:base + NOPE_DIM] = (q[:, k * QK_PAD:k * QK_PAD + NOPE_DIM] * Q_SCALE).astype(BF16)
            q_out[:, base + NOPE_DIM:base + QK_PAD] = _rot(
                q[:, k * QK_PAD + NOPE_DIM:(k + 1) * QK_PAD], cq, sq_lo, sq_hi).astype(BF16)
            k_out[:, base:base + NOPE_DIM] = kn[:, k * NOPE_DIM:(k + 1) * NOPE_DIM]
            k_out[:, base + NOPE_DIM:base + QK_PAD] = kr
        mq_out[:, cols] = (mq * MQ_SCALE).astype(BF16)
        gate_out[:, gcols] = jax.nn.sigmoid(merge_gate + bgate_ref[:, gcols]).astype(BF16)


def _mix_in(x, seq, consts, tables):
    t = x.shape[0]
    tm = MIX_TOKENS
    n_s = seq // tm
    bsz = t // seq

    def row(width):
        return pl.BlockSpec((tm, width), lambda b, i: (b * n_s + i, 0))

    def tab(width):
        return pl.BlockSpec((tm, width), lambda b, i: (i, 0))

    row_outs = [(MLA_HEADS * QK_PAD, BF16), (MLA_HEADS * QK_PAD, BF16)]
    tail_outs = [(RNN_WIDTH, BF16), (MEM_WIDTH, BF16), (N_BRANCH * D_MODEL, BF16)]
    vt_spec = pl.BlockSpec((MLA_HEADS * V_DIM, tm), lambda b, i: (b, i))
    vt_shape = jax.ShapeDtypeStruct((bsz * MLA_HEADS * V_DIM, seq), BF16)
    return pl.pallas_call(
        _mix_in_body,
        grid=(bsz, n_s),
        in_specs=[row(D_MODEL)] + [_const_spec(c.shape) for c in consts] + [tab(LANES)] * 6,
        out_specs=[row(w) for w, _ in row_outs] + [vt_spec] + [row(w) for w, _ in tail_outs],
        out_shape=[jax.ShapeDtypeStruct((t, w), dt) for w, dt in row_outs] + [vt_shape]
                  + [jax.ShapeDtypeStruct((t, w), dt) for w, dt in tail_outs],
        scratch_shapes=[pltpu.VMEM((HALO + tm, RNN_WIDTH), F32), pltpu.VMEM((1, RNN_WIDTH), F32),
                        pltpu.VMEM((RNN_BLOCKS, SCAN_NSEG * SCAN_PITCH, RNN_BLOCK_DIM), F32),
                        pltpu.VMEM((RNN_BLOCKS, SCAN_NSEG * SCAN_PITCH, RNN_BLOCK_DIM), F32)],
        compiler_params=_params("parallel", "arbitrary"),
        name="mix_in",
    )(x, *consts, *tables)


def _col_reduce(op, x):
    n, w = x.shape
    return op(op(x.reshape(n // ATTN_SLAB, ATTN_SLAB, w), axis=0), axis=0, keepdims=True)


def _attn_body(q_ref, k_ref, vt_ref, o_ref, s_buf, p_buf):
    seq = q_ref.shape[0]
    tq = ATTN_Q
    key_chunk = lax.broadcasted_iota(jnp.int32, (tq, tq), 0) // CHUNK
    qry_chunk = lax.broadcasted_iota(jnp.int32, (tq, tq), 1) // CHUNK
    diag_mask = key_chunk <= qry_chunk

    def scores(i):
        lo, hi = i * tq, (i + 1) * tq
        q = q_ref[lo:hi, :]
        s_buf[i % 2, lo:hi, :] = jnp.where(diag_mask, _dot_nt(k_ref[lo:hi, :], q), -jnp.inf)
        for kb in range(i):
            s_buf[i % 2, kb * tq:(kb + 1) * tq, :] = _dot_nt(k_ref[kb * tq:(kb + 1) * tq, :], q)

    def softmax_pv(i):
        lo, hi = i * tq, (i + 1) * tq
        slot = i % 2
        m = _col_reduce(jnp.max, s_buf[slot, 0:hi, :])
        pt = jnp.exp2(s_buf[slot, 0:hi, :] - m)
        l = _col_reduce(jnp.sum, pt)
        p_buf[slot, 0:hi, :] = pt.astype(BF16)
        acc = _dot(vt_ref[:, 0:hi], p_buf[slot, 0:hi, :])
        o_ref[lo:hi, :] = (acc / l).T.astype(BF16)

    order = list(reversed(range(seq // tq)))
    scores(order[0])
    for pos, i in enumerate(order):
        if pos + 1 < len(order):
            scores(order[pos + 1])
        softmax_pv(i)


def _mla_attention(q, k, vt, seq):
    t = q.shape[0]
    bsz = t // seq
    return pl.pallas_call(
        _attn_body,
        grid=(bsz, MLA_HEADS),
        in_specs=[pl.BlockSpec((seq, QK_PAD), lambda b, h: (b, h)),
                  pl.BlockSpec((seq, QK_PAD), lambda b, h: (b, h)),
                  pl.BlockSpec((V_DIM, seq), lambda b, h: (b * MLA_HEADS + h, 0))],
        out_specs=pl.BlockSpec((seq, V_DIM), lambda b, h: (b, h)),
        out_shape=jax.ShapeDtypeStruct((t, MLA_HEADS * V_DIM), BF16),
        scratch_shapes=[pltpu.VMEM((2, seq, ATTN_Q), F32), pltpu.VMEM((2, seq, ATTN_Q), BF16)],
        compiler_params=_params("parallel", "parallel"),
        name="mla_attn",
    )(q, k, vt)


def _memkv_body(m_ref, g_ref, wkv_ref, k_out, v_out):
    h = _rms(m_ref[...], g_ref[...]).astype(BF16)
    k_out[...] = _dot(h, wkv_ref[:, :MEM_WIDTH]).astype(BF16)
    v_out[...] = _dot(h, wkv_ref[:, MEM_WIDTH:]).astype(BF16)


def _memkv(mem, g, wkv):
    t = mem.shape[0]
    row = pl.BlockSpec((N_MEM, D_MODEL), lambda b: (b, 0))
    out = pl.BlockSpec((N_MEM, MEM_WIDTH), lambda b: (b, 0))
    return pl.pallas_call(
        _memkv_body,
        grid=(t // N_MEM,),
        in_specs=[row, _const_spec(g.shape), _const_spec(wkv.shape)],
        out_specs=[out, out],
        out_shape=[jax.ShapeDtypeStruct((t, MEM_WIDTH), BF16)] * 2,
        compiler_params=_params("parallel"),
        name="memkv",
    )(mem, g, wkv)


def _merge_body(x_ref, ya_ref, yb_ref, mq_ref, gate_ref, mk_ref, mv_ref, wb_ref, wo_ref, o_ref):
    heads = [slice(hd * MEM_HEAD_DIM, (hd + 1) * MEM_HEAD_DIM) for hd in range(MEM_HEADS)]
    scores = [_dot_nt(mq_ref[:, sl], mk_ref[:, sl]) for sl in heads]
    proj_a = _dot(ya_ref[...], wb_ref[0])
    probs = []
    for s in scores:
        p = jnp.exp2(s - jnp.max(s, axis=-1, keepdims=True))
        probs.append((p.astype(BF16), jnp.sum(p, axis=-1, keepdims=True)))
    ycs = [(_dot(p, mv_ref[:, sl]) / l).astype(BF16) for (p, l), sl in zip(probs, heads)]
    proj_b = _dot(yb_ref[...], wb_ref[1])
    proj_c = _dot(jnp.concatenate(ycs, axis=1), wb_ref[2])

    merged = None
    for n, proj in enumerate((proj_a, proj_b, proj_c)):
        term = gate_ref[:, n * D_MODEL:(n + 1) * D_MODEL].astype(F32) * proj
        merged = term if merged is None else merged + term
    o_ref[...] = x_ref[...] + _dot(merged.astype(BF16), wo_ref[...])


def _merge(x, ya, yb, mq, gates, mk, mv, wb, wo, seq):
    t = x.shape[0]
    tm = MERGE_TOKENS
    n_s = seq // tm

    def row(width):
        return pl.BlockSpec((tm, width), lambda i: (i, 0))

    memspec = pl.BlockSpec((N_MEM, MEM_WIDTH), lambda i: (i // n_s, 0))
    return pl.pallas_call(
        _merge_body,
        grid=(t // tm,),
        in_specs=[row(D_MODEL), row(D_MODEL), row(D_MODEL), row(MEM_WIDTH), row(N_BRANCH * D_MODEL),
                  memspec, memspec, _const_spec(wb.shape), _const_spec(wo.shape)],
        out_specs=row(D_MODEL),
        out_shape=jax.ShapeDtypeStruct((t, D_MODEL), F32),
        compiler_params=_params("parallel"),
        name="merge",
    )(x, ya, yb, mq, gates, mk, mv, wb, wo)


def _rotary_tables(seq):
    pos = np.arange(seq, dtype=np.float32)
    inv_freq = np.float32(1.0) / (np.float32(ROPE_THETA) ** (np.arange(0, ROPE_DIM, 2, dtype=np.float32) / np.float32(ROPE_DIM)))
    ang = pos[:, None] * inv_freq[None, :]
    cos, sin = np.cos(ang).astype(np.float32), np.sin(ang).astype(np.float32)
    zero = np.zeros_like(cos)

    def lay(a, b):
        return np.concatenate([a, b, np.zeros((seq, LANES - 2 * HALF_ROPE), np.float32)], axis=1)

    k_tabs = (lay(cos, cos), lay(-sin, zero), lay(zero, sin))
    q_tabs = tuple(t * np.float32(Q_SCALE) for t in k_tabs)
    return tuple(jnp.asarray(t) for t in q_tabs + k_tabs)


def _pad_heads(w, keep, per_head, pad_to):
    fan_in = w.shape[0]
    w = w.reshape(fan_in, -1, per_head)[:, :, :keep]
    w = jnp.pad(w, ((0, 0), (0, 0), (0, pad_to - keep)))
    return w.reshape(fan_in, -1)


def kernel(x, mem, ffn1_norm, ffn1_w_in, ffn1_w_down, mix_norm, w_in, b_gate, q_norm, w_uq, kv_norm, w_ukv,
           conv_w, conv_b, w_rg_a, b_rg_a, w_rg_i, b_rg_i, lru_lambda, mem_norm, w_mem_kv, w_branch, w_out,
           ffn2_norm, ffn2_w_in, ffn2_w_down, final_norm):
    bsz, seq, _ = x.shape
    depth = ffn1_norm.shape[0]
    xt = x.reshape(bsz * seq, D_MODEL)
    memt = mem.reshape(bsz * N_MEM, D_MODEL)
    tables = _rotary_tables(seq)
    fn = final_norm.reshape(1, D_MODEL)
    bf = lambda a: a.astype(BF16)

    for l in range(depth):
        last = l == depth - 1
        xt = _ffn(xt, ffn1_norm[l].reshape(1, -1), bf(ffn1_w_in[l]), bf(ffn1_w_down[l]), fn, final=False)

        w = w_in[l]
        c0 = Q_LORA + KV_LORA + ROPE_DIM
        c1 = c0 + 2 * RNN_WIDTH
        wlat = bf(jnp.pad(w[:, :c0], ((0, 0), (0, LAT_WIDTH - c0))))
        wx = bf(w[:, c0:c0 + RNN_WIDTH])
        wg = bf(w[:, c0 + RNN_WIDTH:c1])
        wmq = bf(w[:, c1:c1 + MEM_WIDTH])
        wgate = bf(w[:, c1 + MEM_WIDTH:])
        wuq = bf(_pad_heads(w_uq[l], NOPE_DIM + ROPE_DIM, NOPE_DIM + ROPE_DIM, QK_PAD))
        wukv = w_ukv[l].reshape(KV_LORA, MLA_HEADS, NOPE_DIM + V_DIM)
        wuk = bf(wukv[:, :, :NOPE_DIM].reshape(KV_LORA, -1))
        wuvt = bf(wukv[:, :, NOPE_DIM:].reshape(KV_LORA, -1).T)
        wai = bf(jnp.concatenate([w_rg_a[l], w_rg_i[l]], axis=-1))
        bai = jnp.concatenate([b_rg_a[l], b_rg_i[l]], axis=-1).reshape(RNN_BLOCKS, 1, 2 * RNN_BLOCK_DIM)
        consts = [mix_norm[l].reshape(1, -1), wlat, wx, wg, wmq, wgate, b_gate[l].reshape(1, -1),
                  q_norm[l].reshape(1, -1), wuq, kv_norm[l].reshape(1, -1), wuk, wuvt,
                  conv_w[l].reshape(CONV_WIDTH, RNN_WIDTH), conv_b[l].reshape(1, -1), wai, bai,
                  lru_lambda[l].reshape(1, -1)]
        q, k, vt, yb, mq, gates = _mix_in(xt, seq, consts, tables)

        ya = _mla_attention(q, k, vt, seq)
        mk, mv = _memkv(memt, mem_norm[l].reshape(1, -1), bf(w_mem_kv[l]))
        xt = _merge(xt, ya, yb, mq, gates, mk, mv, bf(w_branch[l]), bf(w_out[l]), seq)

        xt = _ffn(xt, ffn2_norm[l].reshape(1, -1), bf(ffn2_w_in[l]), bf(ffn2_w_down[l]), fn, final=last)
    return xt.reshape(bsz, seq, D_MODEL)
```

```python
import functools

import jax
import jax.numpy as jnp
import numpy as np
from jax import lax
from jax.experimental import pallas as pl
from jax.experimental.pallas import tpu as pltpu

F32 = jnp.float32
BF16 = jnp.bfloat16

D_MODEL = 1024
CHUNK = 64
N_MEM = 256
RMS_EPS = 1e-6
FFN_HIDDEN = 2816
MLA_HEADS = 8
Q_LORA = 384
KV_LORA = 256
NOPE_DIM = 128
ROPE_DIM = 64
V_DIM = 128
ROPE_THETA = 10000.0
RNN_WIDTH = 1024
RNN_BLOCKS = 8
RNN_BLOCK_DIM = RNN_WIDTH // RNN_BLOCKS
CONV_WIDTH = 4
LRU_C = 8.0
MEM_HEADS = 4
MEM_HEAD_DIM = 256
MEM_WIDTH = MEM_HEADS * MEM_HEAD_DIM
N_BRANCH = 3

LOG2E = float(np.log2(np.e))
Q_SCALE = (NOPE_DIM + ROPE_DIM) ** -0.5 * LOG2E
MQ_SCALE = MEM_HEAD_DIM ** -0.5 * LOG2E
HALF_ROPE = ROPE_DIM // 2
LANES = 128
QK_PAD = 256
LAT_WIDTH = 768
WIN_X = LAT_WIDTH
WIN_G = WIN_X + RNN_WIDTH
WIN_MQ = WIN_G + RNN_WIDTH
WIN_GATE = WIN_MQ + MEM_WIDTH
HALO = 8

VMEM_LIMIT_BYTES = 56 * 1024 * 1024

FFN_TOKENS = 1024
FFN_PARTS = 4
FFN_CHUNKS = ((0, 1024), (1024, 2048), (2048, FFN_HIDDEN))
MIX_TOKENS = 512
MIX_GROUPS = 4
MIX_GROUP_WIDTH = RNN_WIDTH // MIX_GROUPS
SCAN_NSEG = 8
SCAN_SEG = MIX_TOKENS // SCAN_NSEG
SCAN_PITCH = SCAN_SEG + 8
ATTN_Q = 256
ATTN_BOUND_MARGIN = 1.05
ATTN_BOUND_LIMIT = 60.0
ATTN_SLAB = 64
MERGE_TOKENS = 512
MEMKV_TOKENS = 1024


def _dot(a, b):
    return jnp.dot(a, b, preferred_element_type=F32)


def _dot_nt(a, b):
    return lax.dot_general(a, b, (((1,), (1,)), ((), ())), preferred_element_type=F32)


def _rms(x, g):
    return x * lax.rsqrt(jnp.mean(x * x, axis=-1, keepdims=True) + RMS_EPS) * g


def _const_spec(shape):
    nd = len(shape)
    return pl.BlockSpec(shape, lambda *_: (0,) * nd, pipeline_mode=pl.Buffered(1))


def _params(*sem):
    return pltpu.CompilerParams(dimension_semantics=sem, vmem_limit_bytes=VMEM_LIMIT_BYTES)


def _ffn_body(x_ref, g_ref, wgu_ref, wd_ref, fn_ref, o_ref, *, final):
    part_rows = x_ref.shape[0] // FFN_PARTS
    for r in range(FFN_PARTS):
        rows = slice(r * part_rows, (r + 1) * part_rows)
        x = x_ref[rows, :]
        h = _rms(x, g_ref[...]).astype(BF16)
        acc = None
        for lo, hi in FFN_CHUNKS:
            gate = _dot(h, wgu_ref[:, lo:hi])
            up = _dot(h, wgu_ref[:, FFN_HIDDEN + lo:FFN_HIDDEN + hi])
            a = (gate * jax.nn.sigmoid(gate) * up).astype(BF16)
            part = _dot(a, wd_ref[lo:hi, :])
            acc = part if acc is None else acc + part
        y = x + 0.5 * acc
        if final:
            y = _rms(y, fn_ref[...])
        o_ref[rows, :] = y


def _ffn(x, g, wgu, wd, fn, *, final):
    t = x.shape[0]
    tm = FFN_TOKENS
    row = pl.BlockSpec((tm, D_MODEL), lambda i: (i, 0))
    return pl.pallas_call(
        functools.partial(_ffn_body, final=final),
        grid=(t // tm,),
        in_specs=[row, _const_spec((1, D_MODEL)), _const_spec(wgu.shape), _const_spec(wd.shape),
                  _const_spec((1, D_MODEL))],
        out_specs=row,
        out_shape=jax.ShapeDtypeStruct((t, D_MODEL), F32),
        compiler_params=_params("parallel"),
        name="ffn_final" if final else "ffn",
    )(x, g, wgu, wd, fn)


def _gelu_tanh(x):
    c = float(np.sqrt(2.0 / np.pi))
    inner = x * ((x * x) * (c * 0.044715) + c)
    return x * (0.5 * jnp.tanh(inner) + 0.5)


def _rot(x, c, s_lo, s_hi):
    w = x.shape[-1]
    return x * c + pltpu.roll(x, w - HALF_ROPE, 1) * s_lo + pltpu.roll(x, HALF_ROPE, 1) * s_hi


def _rglru_coeffs(xc, gates, n0, bai_ref, neg_c_log2, pos_c, a_s, u_s):
    for k, g in enumerate(gates):
        n = n0 + k
        loc = slice(k * RNN_BLOCK_DIM, (k + 1) * RNN_BLOCK_DIM)
        blk = slice(n * RNN_BLOCK_DIM, (n + 1) * RNN_BLOCK_DIM)
        g = g + bai_ref[n]
        r = jax.nn.sigmoid(g[:, :RNN_BLOCK_DIM])
        ig = jax.nn.sigmoid(g[:, RNN_BLOCK_DIM:])
        a = jnp.exp2(r * neg_c_log2[:, blk])
        y = jnp.tanh(r * pos_c[:, blk]) * (1.0 + a * a)
        root = jnp.where(y == 0.0, 0.0, y * lax.rsqrt(y))
        u = root * (ig * xc[:, loc])
        for s in range(SCAN_NSEG):
            src = slice(s * SCAN_SEG, (s + 1) * SCAN_SEG)
            dst = slice(s * SCAN_PITCH, s * SCAN_PITCH + SCAN_SEG)
            a_s[n, dst, :] = a[src, :]
            u_s[n, dst, :] = u[src, :]


def _segmented_scan(a_s, u_s, n, h_in):
    h_loc = prod = None
    for j in range(SCAN_SEG):
        rows = pl.ds(j, SCAN_NSEG, stride=SCAN_PITCH)
        aj, uj = a_s[n, rows, :], u_s[n, rows, :]
        h_loc = uj if j == 0 else aj * h_loc + uj
        prod = aj if j == 0 else aj * prod
        u_s[n, rows, :] = h_loc
        a_s[n, rows, :] = prod
    seg_in = []
    for s in range(SCAN_NSEG):
        seg_in.append(h_in)
        last = s * SCAN_PITCH + SCAN_SEG - 1
        h_in = a_s[n, last:last + 1, :] * h_in + u_s[n, last:last + 1, :]
    return seg_in, h_in


def _mix_in_body(x_ref, gmix_ref, win_ref, bgate_ref,
                 qnorm_ref, wuq_ref, kvnorm_ref, wuk_ref, wuvt_ref,
                 cw_ref, cb_ref, wai_ref, bai_ref, lam_ref,
                 cq_ref, sq_lo_ref, sq_hi_ref, ck_ref, sk_lo_ref, sk_hi_ref,
                 q_out, k_out, vt_out, yb_out, mq_out, gate_out, qsq_out, ksq_out,
                 xbuf, hstate, a_s, u_s):
    ts = MIX_TOKENS

    @pl.when(pl.program_id(1) == 0)
    def _():
        xbuf[0:HALO, :] = jnp.zeros((HALO, RNN_WIDTH), F32)
        hstate[...] = jnp.zeros((1, RNN_WIDTH), F32)

    h = _rms(x_ref[...], gmix_ref[...]).astype(BF16)
    lat = _dot(h, win_ref[:, :LAT_WIDTH])
    hq = _rms(lat[:, :Q_LORA], qnorm_ref[...]).astype(BF16)
    hkv = _rms(lat[:, Q_LORA:Q_LORA + KV_LORA], kvnorm_ref[...]).astype(BF16)
    kr = _rot(lat[:, Q_LORA + KV_LORA:], ck_ref[...], sk_lo_ref[...], sk_hi_ref[...])
    kr_sq = jnp.sum(kr * kr, axis=1, keepdims=True)
    kr = kr.astype(BF16)
    cq, sq_lo, sq_hi = cq_ref[...], sq_lo_ref[...], sq_hi_ref[...]

    lam = lam_ref[...]
    sp = jnp.maximum(-lam, 0.0) + jnp.log1p(jnp.exp(-jnp.abs(lam)))
    neg_c_log2 = (-LRU_C * LOG2E) * sp
    pos_c = LRU_C * sp

    blocks = MIX_GROUP_WIDTH // RNN_BLOCK_DIM
    heads = MIX_GROUP_WIDTH // NOPE_DIM
    for grp in range(MIX_GROUPS):
        c0, c1 = grp * MIX_GROUP_WIDTH, (grp + 1) * MIX_GROUP_WIDTH
        cols = slice(c0, c1)
        gcols = slice(grp * N_BRANCH * MIX_GROUP_WIDTH, (grp + 1) * N_BRANCH * MIX_GROUP_WIDTH)
        n0 = grp * blocks

        xbuf[HALO:HALO + ts, cols] = _dot(h, win_ref[:, WIN_X + c0:WIN_X + c1])
        q = _dot(hq, wuq_ref[:, grp * heads * QK_PAD:(grp + 1) * heads * QK_PAD])
        kn = _dot(hkv, wuk_ref[:, cols])
        vt_out[cols, :] = _dot_nt(wuvt_ref[cols, :], hkv).astype(BF16)

        xc = cb_ref[:, cols] + cw_ref[CONV_WIDTH - 1:CONV_WIDTH, cols] * xbuf[HALO:HALO + ts, cols]
        for d in range(1, CONV_WIDTH):
            xc = xc + cw_ref[CONV_WIDTH - 1 - d:CONV_WIDTH - d, cols] * xbuf[HALO - d:HALO - d + ts, cols]
        xbuf[0:HALO, cols] = xbuf[ts:ts + HALO, cols]
        xcb = xc.astype(BF16)
        gates = [_dot(xcb[:, k * RNN_BLOCK_DIM:(k + 1) * RNN_BLOCK_DIM], wai_ref[n0 + k]) for k in range(blocks)]

        mq = _dot(h, win_ref[:, WIN_MQ + c0:WIN_MQ + c1])
        gz = _dot(h, win_ref[:, WIN_G + c0:WIN_G + c1])
        merge_gate = _dot(h, win_ref[:, WIN_GATE + N_BRANCH * c0:WIN_GATE + N_BRANCH * c1])

        _rglru_coeffs(xc, gates, n0, bai_ref, neg_c_log2, pos_c, a_s, u_s)
        gz = _gelu_tanh(gz)
        for k in range(blocks):
            n = n0 + k
            blk = slice(n * RNN_BLOCK_DIM, (n + 1) * RNN_BLOCK_DIM)
            seg_in, h_last = _segmented_scan(a_s, u_s, n, hstate[:, blk])
            hstate[:, blk] = h_last
            for s, h_seg in enumerate(seg_in):
                rows = slice(s * SCAN_SEG, (s + 1) * SCAN_SEG)
                pad = slice(s * SCAN_PITCH, s * SCAN_PITCH + SCAN_SEG)
                h_true = u_s[n, pad, :] + a_s[n, pad, :] * h_seg
                yb_out[rows, blk] = (h_true * gz[rows, k * RNN_BLOCK_DIM:(k + 1) * RNN_BLOCK_DIM]).astype(BF16)

        for k in range(heads):
            hd = grp * heads + k
            base = hd * QK_PAD
            q_nope = q[:, k * QK_PAD:k * QK_PAD + NOPE_DIM] * Q_SCALE
            q_rope = _rot(q[:, k * QK_PAD + NOPE_DIM:(k + 1) * QK_PAD], cq, sq_lo, sq_hi)
            k_nope = kn[:, k * NOPE_DIM:(k + 1) * NOPE_DIM]
            q_out[:, base:base + NOPE_DIM] = q_nope.astype(BF16)
            q_out[:, base + NOPE_DIM:base + QK_PAD] = q_rope.astype(BF16)
            k_out[:, base:base + NOPE_DIM] = k_nope.astype(BF16)
            k_out[:, base + NOPE_DIM:base + QK_PAD] = kr
            q_sq = jnp.sum(q_nope * q_nope + q_rope * q_rope, axis=1, keepdims=True)
            k_sq = jnp.sum(k_nope * k_nope, axis=1, keepdims=True) + kr_sq
            qsq_out[0, 0, hd:hd + 1, :] = jnp.broadcast_to(jnp.max(q_sq, axis=0, keepdims=True), (1, LANES))
            ksq_out[0, 0, hd:hd + 1, :] = jnp.broadcast_to(jnp.max(k_sq, axis=0, keepdims=True), (1, LANES))
        mq_out[:, cols] = (mq * MQ_SCALE).astype(BF16)
        gate_out[:, gcols] = jax.nn.sigmoid(merge_gate + bgate_ref[:, gcols]).astype(BF16)


def _mix_in(x, seq, consts, tables):
    t = x.shape[0]
    tm = MIX_TOKENS
    n_s = seq // tm
    bsz = t // seq

    def row(width):
        return pl.BlockSpec((tm, width), lambda b, i: (b * n_s + i, 0))

    def tab(width):
        return pl.BlockSpec((tm, width), lambda b, i: (i, 0))

    row_outs = [(MLA_HEADS * QK_PAD, BF16), (MLA_HEADS * QK_PAD, BF16)]
    tail_outs = [(RNN_WIDTH, BF16), (MEM_WIDTH, BF16), (N_BRANCH * D_MODEL, BF16)]
    vt_spec = pl.BlockSpec((MLA_HEADS * V_DIM, tm), lambda b, i: (b, i))
    vt_shape = jax.ShapeDtypeStruct((bsz * MLA_HEADS * V_DIM, seq), BF16)
    sq_spec = pl.BlockSpec((1, 1, MLA_HEADS, LANES), lambda b, i: (b, i, 0, 0))
    sq_shape = jax.ShapeDtypeStruct((bsz, n_s, MLA_HEADS, LANES), F32)
    return pl.pallas_call(
        _mix_in_body,
        grid=(bsz, n_s),
        in_specs=[row(D_MODEL)] + [_const_spec(c.shape) for c in consts] + [tab(LANES)] * 6,
        out_specs=[row(w) for w, _ in row_outs] + [vt_spec] + [row(w) for w, _ in tail_outs] + [sq_spec] * 2,
        out_shape=[jax.ShapeDtypeStruct((t, w), dt) for w, dt in row_outs] + [vt_shape]
                  + [jax.ShapeDtypeStruct((t, w), dt) for w, dt in tail_outs] + [sq_shape] * 2,
        scratch_shapes=[pltpu.VMEM((HALO + tm, RNN_WIDTH), F32), pltpu.VMEM((1, RNN_WIDTH), F32),
                        pltpu.VMEM((RNN_BLOCKS, SCAN_NSEG * SCAN_PITCH, RNN_BLOCK_DIM), F32),
                        pltpu.VMEM((RNN_BLOCKS, SCAN_NSEG * SCAN_PITCH, RNN_BLOCK_DIM), F32)],
        compiler_params=_params("parallel", "arbitrary"),
        name="mix_in",
    )(x, *consts, *tables)


def _col_reduce(op, x):
    n, w = x.shape
    return op(op(x.reshape(n // ATTN_SLAB, ATTN_SLAB, w), axis=0), axis=0, keepdims=True)


def _attn_body(q_ref, k_ref, vt_ref, qsq_ref, ksq_ref, o_ref, s_buf, p_buf):
    seq = q_ref.shape[0]
    tq = ATTN_Q
    key_chunk = lax.broadcasted_iota(jnp.int32, (tq, tq), 0) // CHUNK
    qry_chunk = lax.broadcasted_iota(jnp.int32, (tq, tq), 1) // CHUNK
    diag_mask = key_chunk <= qry_chunk
    order = list(reversed(range(seq // tq)))

    def finish(i, l):
        lo, hi = i * tq, (i + 1) * tq
        acc = _dot(vt_ref[:, 0:hi], p_buf[i % 2, 0:hi, :])
        o_ref[lo:hi, :] = (acc / l).T.astype(BF16)

    head_row = lax.broadcasted_iota(jnp.int32, (MLA_HEADS, LANES), 0) == pl.program_id(1)

    def head_max(ref):
        vals = jnp.max(ref[0], axis=0)
        return jnp.max(jnp.where(head_row, vals, 0.0), axis=0, keepdims=True)

    bound = jnp.sqrt(head_max(qsq_ref) * head_max(ksq_ref)) * ATTN_BOUND_MARGIN
    small = bound[0, 0] <= ATTN_BOUND_LIMIT

    @pl.when(small)
    def _():
        shift = jnp.concatenate([bound] * (tq // LANES), axis=1)
        blocks = [(i, kb) for i in order for kb in range(i + 1)]

        def score(i, kb):
            st = _dot_nt(k_ref[kb * tq:(kb + 1) * tq, :], q_ref[i * tq:(i + 1) * tq, :])
            return jnp.where(diag_mask, st, -jnp.inf) if kb == i else st

        st = score(*blocks[0])
        l = None
        for n, (i, kb) in enumerate(blocks):
            st_next = score(*blocks[n + 1]) if n + 1 < len(blocks) else None
            pt = jnp.exp2(st - shift)
            part = _col_reduce(jnp.sum, pt)
            l = part if kb == 0 else l + part
            p_buf[i % 2, kb * tq:(kb + 1) * tq, :] = pt.astype(BF16)
            if kb == i:
                finish(i, l)
            st = st_next

    @pl.when(jnp.logical_not(small))
    def _():
        def scores(i):
            lo, hi = i * tq, (i + 1) * tq
            q = q_ref[lo:hi, :]
            s_buf[i % 2, lo:hi, :] = jnp.where(diag_mask, _dot_nt(k_ref[lo:hi, :], q), -jnp.inf)
            for kb in range(i):
                s_buf[i % 2, kb * tq:(kb + 1) * tq, :] = _dot_nt(k_ref[kb * tq:(kb + 1) * tq, :], q)

        def softmax_pv(i):
            hi = (i + 1) * tq
            m = _col_reduce(jnp.max, s_buf[i % 2, 0:hi, :])
            pt = jnp.exp2(s_buf[i % 2, 0:hi, :] - m)
            l = _col_reduce(jnp.sum, pt)
            p_buf[i % 2, 0:hi, :] = pt.astype(BF16)
            finish(i, l)

        scores(order[0])
        for pos, i in enumerate(order):
            if pos + 1 < len(order):
                scores(order[pos + 1])
            softmax_pv(i)


def _mla_attention(q, k, vt, qsq, ksq, seq):
    t = q.shape[0]
    bsz = t // seq
    sq_spec = pl.BlockSpec((1,) + qsq.shape[1:], lambda b, h: (b, 0, 0, 0))
    return pl.pallas_call(
        _attn_body,
        grid=(bsz, MLA_HEADS),
        in_specs=[pl.BlockSpec((seq, QK_PAD), lambda b, h: (b, h)),
                  pl.BlockSpec((seq, QK_PAD), lambda b, h: (b, h)),
                  pl.BlockSpec((V_DIM, seq), lambda b, h: (b * MLA_HEADS + h, 0)),
                  sq_spec, sq_spec],
        out_specs=pl.BlockSpec((seq, V_DIM), lambda b, h: (b, h)),
        out_shape=jax.ShapeDtypeStruct((t, MLA_HEADS * V_DIM), BF16),
        scratch_shapes=[pltpu.VMEM((2, seq, ATTN_Q), F32), pltpu.VMEM((2, seq, ATTN_Q), BF16)],
        compiler_params=_params("parallel", "parallel"),
        name="mla_attn",
    )(q, k, vt, qsq, ksq)


def _memkv_body(m_ref, g_ref, wkv_ref, k_out, v_out):
    h = _rms(m_ref[...], g_ref[...]).astype(BF16)
    k_out[...] = _dot(h, wkv_ref[:, :MEM_WIDTH]).astype(BF16)
    v_out[...] = _dot(h, wkv_ref[:, MEM_WIDTH:]).astype(BF16)


def _memkv(mem, g, wkv):
    t = mem.shape[0]
    tm = MEMKV_TOKENS
    row = pl.BlockSpec((tm, D_MODEL), lambda b: (b, 0))
    out = pl.BlockSpec((tm, MEM_WIDTH), lambda b: (b, 0))
    return pl.pallas_call(
        _memkv_body,
        grid=(t // tm,),
        in_specs=[row, _const_spec(g.shape), _const_spec(wkv.shape)],
        out_specs=[out, out],
        out_shape=[jax.ShapeDtypeStruct((t, MEM_WIDTH), BF16)] * 2,
        compiler_params=_params("parallel"),
        name="memkv",
    )(mem, g, wkv)


def _merge_body(x_ref, ya_ref, yb_ref, mq_ref, gate_ref, mk_ref, mv_ref, wb_ref, wo_ref, o_ref):
    heads = [slice(hd * MEM_HEAD_DIM, (hd + 1) * MEM_HEAD_DIM) for hd in range(MEM_HEADS)]
    scores = [_dot_nt(mq_ref[:, sl], mk_ref[:, sl]) for sl in heads]
    proj_a = _dot(ya_ref[...], wb_ref[0])
    probs = []
    for s in scores:
        p = jnp.exp2(s - jnp.max(s, axis=-1, keepdims=True))
        probs.append((p.astype(BF16), jnp.sum(p, axis=-1, keepdims=True)))
    ycs = [(_dot(p, mv_ref[:, sl]) / l).astype(BF16) for (p, l), sl in zip(probs, heads)]
    proj_b = _dot(yb_ref[...], wb_ref[1])
    proj_c = _dot(jnp.concatenate(ycs, axis=1), wb_ref[2])

    merged = None
    for n, proj in enumerate((proj_a, proj_b, proj_c)):
        term = gate_ref[:, n * D_MODEL:(n + 1) * D_MODEL].astype(F32) * proj
        merged = term if merged is None else merged + term
    o_ref[...] = x_ref[...] + _dot(merged.astype(BF16), wo_ref[...])


def _merge(x, ya, yb, mq, gates, mk, mv, wb, wo, seq):
    t = x.shape[0]
    tm = MERGE_TOKENS
    n_s = seq // tm

    def row(width):
        return pl.BlockSpec((tm, width), lambda i: (i, 0))

    memspec = pl.BlockSpec((N_MEM, MEM_WIDTH), lambda i: (i // n_s, 0))
    return pl.pallas_call(
        _merge_body,
        grid=(t // tm,),
        in_specs=[row(D_MODEL), row(D_MODEL), row(D_MODEL), row(MEM_WIDTH), row(N_BRANCH * D_MODEL),
                  memspec, memspec, _const_spec(wb.shape), _const_spec(wo.shape)],
        out_specs=row(D_MODEL),
        out_shape=jax.ShapeDtypeStruct((t, D_MODEL), F32),
        compiler_params=_params("parallel"),
        name="merge",
    )(x, ya, yb, mq, gates, mk, mv, wb, wo)


def _rotary_tables(seq):
    pos = np.arange(seq, dtype=np.float32)
    inv_freq = np.float32(1.0) / (np.float32(ROPE_THETA) ** (np.arange(0, ROPE_DIM, 2, dtype=np.float32) / np.float32(ROPE_DIM)))
    ang = pos[:, None] * inv_freq[None, :]
    cos, sin = np.cos(ang).astype(np.float32), np.sin(ang).astype(np.float32)
    zero = np.zeros_like(cos)

    def lay(a, b):
        return np.concatenate([a, b, np.zeros((seq, LANES - 2 * HALF_ROPE), np.float32)], axis=1)

    k_tabs = (lay(cos, cos), lay(-sin, zero), lay(zero, sin))
    q_tabs = tuple(t * np.float32(Q_SCALE) for t in k_tabs)
    return tuple(jnp.asarray(t) for t in q_tabs + k_tabs)


def _pad_heads(w, keep, per_head, pad_to):
    fan_in = w.shape[0]
    w = w.reshape(fan_in, -1, per_head)[:, :, :keep]
    w = jnp.pad(w, ((0, 0), (0, 0), (0, pad_to - keep)))
    return w.reshape(fan_in, -1)


def kernel(x, mem, ffn1_norm, ffn1_w_in, ffn1_w_down, mix_norm, w_in, b_gate, q_norm, w_uq, kv_norm, w_ukv,
           conv_w, conv_b, w_rg_a, b_rg_a, w_rg_i, b_rg_i, lru_lambda, mem_norm, w_mem_kv, w_branch, w_out,
           ffn2_norm, ffn2_w_in, ffn2_w_down, final_norm):
    bsz, seq, _ = x.shape
    depth = ffn1_norm.shape[0]
    xt = x.reshape(bsz * seq, D_MODEL)
    memt = mem.reshape(bsz * N_MEM, D_MODEL)
    tables = _rotary_tables(seq)
    fn = final_norm.reshape(1, D_MODEL)
    bf = lambda a: a.astype(BF16)

    for l in range(depth):
        last = l == depth - 1
        xt = _ffn(xt, ffn1_norm[l].reshape(1, -1), bf(ffn1_w_in[l]), bf(ffn1_w_down[l]), fn, final=False)

        c0 = Q_LORA + KV_LORA + ROPE_DIM
        win = bf(jnp.concatenate([w_in[l][:, :c0], jnp.zeros((D_MODEL, LAT_WIDTH - c0), F32), w_in[l][:, c0:]], axis=1))
        wuq = bf(_pad_heads(w_uq[l], NOPE_DIM + ROPE_DIM, NOPE_DIM + ROPE_DIM, QK_PAD))
        wukv = w_ukv[l].reshape(KV_LORA, MLA_HEADS, NOPE_DIM + V_DIM)
        wuk = bf(wukv[:, :, :NOPE_DIM].reshape(KV_LORA, -1))
        wuvt = bf(wukv[:, :, NOPE_DIM:].reshape(KV_LORA, -1).T)
        wai = bf(jnp.concatenate([w_rg_a[l], w_rg_i[l]], axis=-1))
        bai = jnp.concatenate([b_rg_a[l], b_rg_i[l]], axis=-1).reshape(RNN_BLOCKS, 1, 2 * RNN_BLOCK_DIM)
        consts = [mix_norm[l].reshape(1, -1), win, b_gate[l].reshape(1, -1),
                  q_norm[l].reshape(1, -1), wuq, kv_norm[l].reshape(1, -1), wuk, wuvt,
                  conv_w[l].reshape(CONV_WIDTH, RNN_WIDTH), conv_b[l].reshape(1, -1), wai, bai,
                  lru_lambda[l].reshape(1, -1)]
        q, k, vt, yb, mq, gates, qsq, ksq = _mix_in(xt, seq, consts, tables)

        ya = _mla_attention(q, k, vt, qsq, ksq, seq)
        mk, mv = _memkv(memt, mem_norm[l].reshape(1, -1), bf(w_mem_kv[l]))
        xt = _merge(xt, ya, yb, mq, gates, mk, mv, bf(w_branch[l]), bf(w_out[l]), seq)

        xt = _ffn(xt, ffn2_norm[l].reshape(1, -1), bf(ffn2_w_in[l]), bf(ffn2_w_down[l]), fn, final=last)
    return xt.reshape(bsz, seq, D_MODEL)
```

```python
import functools

import jax
import jax.numpy as jnp
import numpy as np
from jax import lax
from jax.experimental import pallas as pl
from jax.experimental.pallas import tpu as pltpu

F32 = jnp.float32
BF16 = jnp.bfloat16

D_MODEL = 1024
CHUNK = 64
N_MEM = 256
RMS_EPS = 1e-6
FFN_HIDDEN = 2816
MLA_HEADS = 8
Q_LORA = 384
KV_LORA = 256
NOPE_DIM = 128
ROPE_DIM = 64
V_DIM = 128
ROPE_THETA = 10000.0
RNN_WIDTH = 1024
RNN_BLOCKS = 8
RNN_BLOCK_DIM = RNN_WIDTH // RNN_BLOCKS
CONV_WIDTH = 4
LRU_C = 8.0
MEM_HEADS = 4
MEM_HEAD_DIM = 256
MEM_WIDTH = MEM_HEADS * MEM_HEAD_DIM
N_BRANCH = 3

LOG2E = float(np.log2(np.e))
Q_SCALE = (NOPE_DIM + ROPE_DIM) ** -0.5 * LOG2E
MQ_SCALE = MEM_HEAD_DIM ** -0.5 * LOG2E
HALF_ROPE = ROPE_DIM // 2
LANES = 128
QK_PAD = 256
LAT_WIDTH = 768
WIN_X = LAT_WIDTH
WIN_G = WIN_X + RNN_WIDTH
WIN_MQ = WIN_G + RNN_WIDTH
WIN_GATE = WIN_MQ + MEM_WIDTH
HALO = 8

VMEM_LIMIT_BYTES = 56 * 1024 * 1024

FFN_TOKENS = 1024
FFN_PARTS = 4
FFN_CHUNKS = ((0, 1024), (1024, 2048), (2048, FFN_HIDDEN))
MIX_TOKENS = 512
MIX_GROUPS = 4
MIX_GROUP_WIDTH = RNN_WIDTH // MIX_GROUPS
SCAN_NSEG = 8
SCAN_SEG = MIX_TOKENS // SCAN_NSEG
SCAN_PITCH = SCAN_SEG + 8
ATTN_Q = 256
ATTN_BOUND_MARGIN = 1.05
ATTN_BOUND_LIMIT = 60.0
ATTN_SLAB = 64
MERGE_TOKENS = 512
MEMKV_TOKENS = 1024


def _dot(a, b):
    return jnp.dot(a, b, preferred_element_type=F32)


def _dot_nt(a, b):
    return lax.dot_general(a, b, (((1,), (1,)), ((), ())), preferred_element_type=F32)


def _rms(x, g):
    return x * lax.rsqrt(jnp.mean(x * x, axis=-1, keepdims=True) + RMS_EPS) * g


def _const_spec(shape):
    nd = len(shape)
    return pl.BlockSpec(shape, lambda *_: (0,) * nd, pipeline_mode=pl.Buffered(1))


def _params(*sem):
    return pltpu.CompilerParams(dimension_semantics=sem, vmem_limit_bytes=VMEM_LIMIT_BYTES)


def _ffn_body(x_ref, g_ref, wgu_ref, wd_ref, fn_ref, o_ref, *, final):
    part_rows = x_ref.shape[0] // FFN_PARTS
    for r in range(FFN_PARTS):
        rows = slice(r * part_rows, (r + 1) * part_rows)
        x = x_ref[rows, :]
        h = _rms(x, g_ref[...]).astype(BF16)
        acc = None
        for lo, hi in FFN_CHUNKS:
            gate = _dot(h, wgu_ref[:, lo:hi])
            up = _dot(h, wgu_ref[:, FFN_HIDDEN + lo:FFN_HIDDEN + hi])
            a = (gate * jax.nn.sigmoid(gate) * up).astype(BF16)
            part = _dot(a, wd_ref[lo:hi, :])
            acc = part if acc is None else acc + part
        y = x + 0.5 * acc
        if final:
            y = _rms(y, fn_ref[...])
        o_ref[rows, :] = y


def _ffn(x, g, wgu, wd, fn, *, final):
    t = x.shape[0]
    tm = FFN_TOKENS
    row = pl.BlockSpec((tm, D_MODEL), lambda i: (i, 0))
    return pl.pallas_call(
        functools.partial(_ffn_body, final=final),
        grid=(t // tm,),
        in_specs=[row, _const_spec((1, D_MODEL)), _const_spec(wgu.shape), _const_spec(wd.shape),
                  _const_spec((1, D_MODEL))],
        out_specs=row,
        out_shape=jax.ShapeDtypeStruct((t, D_MODEL), F32),
        compiler_params=_params("parallel"),
        name="ffn_final" if final else "ffn",
    )(x, g, wgu, wd, fn)


def _gelu_tanh(x):
    c = float(np.sqrt(2.0 / np.pi))
    inner = x * ((x * x) * (c * 0.044715) + c)
    return x * (0.5 * jnp.tanh(inner) + 0.5)


def _rot(x, c, s_lo, s_hi):
    w = x.shape[-1]
    return x * c + pltpu.roll(x, w - HALF_ROPE, 1) * s_lo + pltpu.roll(x, HALF_ROPE, 1) * s_hi


def _rglru_coeffs(xc, gates, n0, bai_ref, neg_c_log2, pos_c, a_s, u_s):
    for k, g in enumerate(gates):
        n = n0 + k
        loc = slice(k * RNN_BLOCK_DIM, (k + 1) * RNN_BLOCK_DIM)
        blk = slice(n * RNN_BLOCK_DIM, (n + 1) * RNN_BLOCK_DIM)
        g = g + bai_ref[n]
        r = jax.nn.sigmoid(g[:, :RNN_BLOCK_DIM])
        ig = jax.nn.sigmoid(g[:, RNN_BLOCK_DIM:])
        a = jnp.exp2(r * neg_c_log2[:, blk])
        y = jnp.tanh(r * pos_c[:, blk]) * (1.0 + a * a)
        root = jnp.where(y == 0.0, 0.0, y * lax.rsqrt(y))
        u = root * (ig * xc[:, loc])
        for s in range(SCAN_NSEG):
            src = slice(s * SCAN_SEG, (s + 1) * SCAN_SEG)
            dst = slice(s * SCAN_PITCH, s * SCAN_PITCH + SCAN_SEG)
            a_s[n, dst, :] = a[src, :]
            u_s[n, dst, :] = u[src, :]


def _segmented_scan(a_s, u_s, n, h_in):
    h_loc = prod = None
    for j in range(SCAN_SEG):
        rows = pl.ds(j, SCAN_NSEG, stride=SCAN_PITCH)
        aj, uj = a_s[n, rows, :], u_s[n, rows, :]
        h_loc = uj if j == 0 else aj * h_loc + uj
        prod = aj if j == 0 else aj * prod
        u_s[n, rows, :] = h_loc
        a_s[n, rows, :] = prod
    seg_in = []
    for s in range(SCAN_NSEG):
        seg_in.append(h_in)
        last = s * SCAN_PITCH + SCAN_SEG - 1
        h_in = a_s[n, last:last + 1, :] * h_in + u_s[n, last:last + 1, :]
    return seg_in, h_in


def _mix_in_body(x_ref, gmix_ref, win_ref, bgate_ref,
                 qnorm_ref, wuq_ref, kvnorm_ref, wuk_ref, wuvt_ref,
                 cw_ref, cb_ref, wai_ref, bai_ref, lam_ref,
                 rc_ref, rlo_ref, rhi_ref,
                 q_out, k_out, vt_out, yb_out, mq_out, gate_out, qsq_out, ksq_out,
                 xbuf, hstate, a_s, u_s):
    ts = MIX_TOKENS

    @pl.when(pl.program_id(1) == 0)
    def _():
        xbuf[0:HALO, :] = jnp.zeros((HALO, RNN_WIDTH), F32)
        hstate[...] = jnp.zeros((1, RNN_WIDTH), F32)

    h = _rms(x_ref[...], gmix_ref[...]).astype(BF16)
    lat = _dot(h, win_ref[:, :LAT_WIDTH])
    hq = _rms(lat[:, :Q_LORA], qnorm_ref[...] * Q_SCALE).astype(BF16)
    hkv = _rms(lat[:, Q_LORA:Q_LORA + KV_LORA], kvnorm_ref[...]).astype(BF16)
    rot_c, rot_lo, rot_hi = rc_ref[...], rlo_ref[...], rhi_ref[...]
    kr = _rot(lat[:, Q_LORA + KV_LORA:], rot_c, rot_lo, rot_hi)
    kr_sq = jnp.sum(kr * kr, axis=1, keepdims=True)
    kr = kr.astype(BF16)

    lam = lam_ref[...]
    sp = jnp.maximum(-lam, 0.0) + jnp.log1p(jnp.exp(-jnp.abs(lam)))
    neg_c_log2 = (-LRU_C * LOG2E) * sp
    pos_c = LRU_C * sp

    blocks = MIX_GROUP_WIDTH // RNN_BLOCK_DIM
    heads = MIX_GROUP_WIDTH // NOPE_DIM
    for grp in range(MIX_GROUPS):
        c0, c1 = grp * MIX_GROUP_WIDTH, (grp + 1) * MIX_GROUP_WIDTH
        cols = slice(c0, c1)
        gcols = slice(grp * N_BRANCH * MIX_GROUP_WIDTH, (grp + 1) * N_BRANCH * MIX_GROUP_WIDTH)
        n0 = grp * blocks

        xbuf[HALO:HALO + ts, cols] = _dot(h, win_ref[:, WIN_X + c0:WIN_X + c1])
        q = _dot(hq, wuq_ref[:, grp * heads * QK_PAD:(grp + 1) * heads * QK_PAD])
        kn = _dot(hkv, wuk_ref[:, cols])
        vt_out[cols, :] = _dot_nt(wuvt_ref[cols, :], hkv).astype(BF16)

        xc = cb_ref[:, cols] + cw_ref[CONV_WIDTH - 1:CONV_WIDTH, cols] * xbuf[HALO:HALO + ts, cols]
        for d in range(1, CONV_WIDTH):
            xc = xc + cw_ref[CONV_WIDTH - 1 - d:CONV_WIDTH - d, cols] * xbuf[HALO - d:HALO - d + ts, cols]
        xbuf[0:HALO, cols] = xbuf[ts:ts + HALO, cols]
        xcb = xc.astype(BF16)
        gates = [_dot(xcb[:, k * RNN_BLOCK_DIM:(k + 1) * RNN_BLOCK_DIM], wai_ref[n0 + k]) for k in range(blocks)]

        mq = _dot(h, win_ref[:, WIN_MQ + c0:WIN_MQ + c1])
        gz = _dot(h, win_ref[:, WIN_G + c0:WIN_G + c1])
        merge_gate = _dot(h, win_ref[:, WIN_GATE + N_BRANCH * c0:WIN_GATE + N_BRANCH * c1])

        _rglru_coeffs(xc, gates, n0, bai_ref, neg_c_log2, pos_c, a_s, u_s)
        gz = _gelu_tanh(gz)
        for k in range(blocks):
            n = n0 + k
            blk = slice(n * RNN_BLOCK_DIM, (n + 1) * RNN_BLOCK_DIM)
            seg_in, h_last = _segmented_scan(a_s, u_s, n, hstate[:, blk])
            hstate[:, blk] = h_last
            for s, h_seg in enumerate(seg_in):
                rows = slice(s * SCAN_SEG, (s + 1) * SCAN_SEG)
                pad = slice(s * SCAN_PITCH, s * SCAN_PITCH + SCAN_SEG)
                h_true = u_s[n, pad, :] + a_s[n, pad, :] * h_seg
                yb_out[rows, blk] = (h_true * gz[rows, k * RNN_BLOCK_DIM:(k + 1) * RNN_BLOCK_DIM]).astype(BF16)

        for k in range(heads):
            hd = grp * heads + k
            base = hd * QK_PAD
            q_nope = q[:, k * QK_PAD:k * QK_PAD + NOPE_DIM]
            q_rope = _rot(q[:, k * QK_PAD + NOPE_DIM:(k + 1) * QK_PAD], rot_c, rot_lo, rot_hi)
            k_nope = kn[:, k * NOPE_DIM:(k + 1) * NOPE_DIM]
            q_out[:, base:base + NOPE_DIM] = q_nope.astype(BF16)
            q_out[:, base + NOPE_DIM:base + QK_PAD] = q_rope.astype(BF16)
            k_out[:, base:base + NOPE_DIM] = k_nope.astype(BF16)
            k_out[:, base + NOPE_DIM:base + QK_PAD] = kr
            q_sq = jnp.sum(q_nope * q_nope + q_rope * q_rope, axis=1, keepdims=True)
            k_sq = jnp.sum(k_nope * k_nope, axis=1, keepdims=True) + kr_sq
            qsq_out[0, 0, hd:hd + 1, :] = jnp.broadcast_to(jnp.max(q_sq, axis=0, keepdims=True), (1, LANES))
            ksq_out[0, 0, hd:hd + 1, :] = jnp.broadcast_to(jnp.max(k_sq, axis=0, keepdims=True), (1, LANES))
        mq_out[:, cols] = mq.astype(BF16)
        gate_out[:, gcols] = (merge_gate + bgate_ref[:, gcols]).astype(BF16)


def _mix_in(x, seq, consts, tables):
    t = x.shape[0]
    tm = MIX_TOKENS
    n_s = seq // tm
    bsz = t // seq

    def row(width):
        return pl.BlockSpec((tm, width), lambda b, i: (b * n_s + i, 0))

    def tab(width):
        return pl.BlockSpec((tm, width), lambda b, i: (i, 0))

    row_outs = [(MLA_HEADS * QK_PAD, BF16), (MLA_HEADS * QK_PAD, BF16)]
    tail_outs = [(RNN_WIDTH, BF16), (MEM_WIDTH, BF16), (N_BRANCH * D_MODEL, BF16)]
    vt_spec = pl.BlockSpec((MLA_HEADS * V_DIM, tm), lambda b, i: (b, i))
    vt_shape = jax.ShapeDtypeStruct((bsz * MLA_HEADS * V_DIM, seq), BF16)
    sq_spec = pl.BlockSpec((1, 1, MLA_HEADS, LANES), lambda b, i: (b, i, 0, 0))
    sq_shape = jax.ShapeDtypeStruct((bsz, n_s, MLA_HEADS, LANES), F32)
    return pl.pallas_call(
        _mix_in_body,
        grid=(bsz, n_s),
        in_specs=[row(D_MODEL)] + [_const_spec(c.shape) for c in consts] + [tab(LANES)] * 3,
        out_specs=[row(w) for w, _ in row_outs] + [vt_spec] + [row(w) for w, _ in tail_outs] + [sq_spec] * 2,
        out_shape=[jax.ShapeDtypeStruct((t, w), dt) for w, dt in row_outs] + [vt_shape]
                  + [jax.ShapeDtypeStruct((t, w), dt) for w, dt in tail_outs] + [sq_shape] * 2,
        scratch_shapes=[pltpu.VMEM((HALO + tm, RNN_WIDTH), F32), pltpu.VMEM((1, RNN_WIDTH), F32),
                        pltpu.VMEM((RNN_BLOCKS, SCAN_NSEG * SCAN_PITCH, RNN_BLOCK_DIM), F32),
                        pltpu.VMEM((RNN_BLOCKS, SCAN_NSEG * SCAN_PITCH, RNN_BLOCK_DIM), F32)],
        compiler_params=_params("parallel", "arbitrary"),
        name="mix_in",
    )(x, *consts, *tables)


def _col_reduce(op, x):
    n, w = x.shape
    return op(op(x.reshape(n // ATTN_SLAB, ATTN_SLAB, w), axis=0), axis=0, keepdims=True)


def _attn_body(q_ref, k_ref, vt_ref, qsq_ref, ksq_ref, o_ref, s_buf, p_buf):
    seq = q_ref.shape[0]
    tq = ATTN_Q
    key_chunk = lax.broadcasted_iota(jnp.int32, (tq, tq), 0) // CHUNK
    qry_chunk = lax.broadcasted_iota(jnp.int32, (tq, tq), 1) // CHUNK
    diag_mask = key_chunk <= qry_chunk
    order = list(reversed(range(seq // tq)))

    def finish(i, l):
        lo, hi = i * tq, (i + 1) * tq
        acc = _dot(vt_ref[:, 0:hi], p_buf[i % 2, 0:hi, :])
        o_ref[lo:hi, :] = (acc / l).T.astype(BF16)

    head_row = lax.broadcasted_iota(jnp.int32, (MLA_HEADS, LANES), 0) == pl.program_id(1)

    def head_max(ref):
        vals = jnp.max(ref[0], axis=0)
        return jnp.max(jnp.where(head_row, vals, 0.0), axis=0, keepdims=True)

    bound = jnp.sqrt(head_max(qsq_ref) * head_max(ksq_ref)) * ATTN_BOUND_MARGIN
    small = bound[0, 0] <= ATTN_BOUND_LIMIT

    @pl.when(small)
    def _():
        shift = jnp.concatenate([bound] * (tq // LANES), axis=1)
        blocks = [(i, kb) for i in order for kb in range(i + 1)]

        def score(i, kb):
            st = _dot_nt(k_ref[kb * tq:(kb + 1) * tq, :], q_ref[i * tq:(i + 1) * tq, :])
            return jnp.where(diag_mask, st, -jnp.inf) if kb == i else st

        st = score(*blocks[0])
        l = None
        for n, (i, kb) in enumerate(blocks):
            st_next = score(*blocks[n + 1]) if n + 1 < len(blocks) else None
            pt = jnp.exp2(st - shift)
            part = _col_reduce(jnp.sum, pt)
            l = part if kb == 0 else l + part
            p_buf[i % 2, kb * tq:(kb + 1) * tq, :] = pt.astype(BF16)
            if kb == i:
                finish(i, l)
            st = st_next

    @pl.when(jnp.logical_not(small))
    def _():
        def scores(i):
            lo, hi = i * tq, (i + 1) * tq
            q = q_ref[lo:hi, :]
            s_buf[i % 2, lo:hi, :] = jnp.where(diag_mask, _dot_nt(k_ref[lo:hi, :], q), -jnp.inf)
            for kb in range(i):
                s_buf[i % 2, kb * tq:(kb + 1) * tq, :] = _dot_nt(k_ref[kb * tq:(kb + 1) * tq, :], q)

        def softmax_pv(i):
            hi = (i + 1) * tq
            m = _col_reduce(jnp.max, s_buf[i % 2, 0:hi, :])
            pt = jnp.exp2(s_buf[i % 2, 0:hi, :] - m)
            l = _col_reduce(jnp.sum, pt)
            p_buf[i % 2, 0:hi, :] = pt.astype(BF16)
            finish(i, l)

        scores(order[0])
        for pos, i in enumerate(order):
            if pos + 1 < len(order):
                scores(order[pos + 1])
            softmax_pv(i)


def _mla_attention(q, k, vt, qsq, ksq, seq):
    t = q.shape[0]
    bsz = t // seq
    sq_spec = pl.BlockSpec((1,) + qsq.shape[1:], lambda b, h: (b, 0, 0, 0))
    return pl.pallas_call(
        _attn_body,
        grid=(bsz, MLA_HEADS),
        in_specs=[pl.BlockSpec((seq, QK_PAD), lambda b, h: (b, h)),
                  pl.BlockSpec((seq, QK_PAD), lambda b, h: (b, h)),
                  pl.BlockSpec((V_DIM, seq), lambda b, h: (b * MLA_HEADS + h, 0)),
                  sq_spec, sq_spec],
        out_specs=pl.BlockSpec((seq, V_DIM), lambda b, h: (b, h)),
        out_shape=jax.ShapeDtypeStruct((t, MLA_HEADS * V_DIM), BF16),
        scratch_shapes=[pltpu.VMEM((2, seq, ATTN_Q), F32), pltpu.VMEM((2, seq, ATTN_Q), BF16)],
        compiler_params=_params("parallel", "parallel"),
        name="mla_attn",
    )(q, k, vt, qsq, ksq)


def _memkv_body(m_ref, g_ref, wkv_ref, k_out, v_out):
    h = _rms(m_ref[...], g_ref[...]).astype(BF16)
    k_out[...] = _dot(h, wkv_ref[:, :MEM_WIDTH]).astype(BF16)
    v_out[...] = _dot(h, wkv_ref[:, MEM_WIDTH:]).astype(BF16)


def _memkv(mem, g, wkv):
    t = mem.shape[0]
    tm = MEMKV_TOKENS
    row = pl.BlockSpec((tm, D_MODEL), lambda b: (b, 0))
    out = pl.BlockSpec((tm, MEM_WIDTH), lambda b: (b, 0))
    return pl.pallas_call(
        _memkv_body,
        grid=(t // tm,),
        in_specs=[row, _const_spec(g.shape), _const_spec(wkv.shape)],
        out_specs=[out, out],
        out_shape=[jax.ShapeDtypeStruct((t, MEM_WIDTH), BF16)] * 2,
        compiler_params=_params("parallel"),
        name="memkv",
    )(mem, g, wkv)


def _merge_body(x_ref, ya_ref, yb_ref, mq_ref, gate_ref, mk_ref, mv_ref, wb_ref, wo_ref, o_ref):
    heads = [slice(hd * MEM_HEAD_DIM, (hd + 1) * MEM_HEAD_DIM) for hd in range(MEM_HEADS)]
    scores = [_dot_nt(mq_ref[:, sl], mk_ref[:, sl]) * MQ_SCALE for sl in heads]
    proj_a = _dot(ya_ref[...], wb_ref[0])
    probs = []
    for s in scores:
        p = jnp.exp2(s - jnp.max(s, axis=-1, keepdims=True))
        probs.append((p.astype(BF16), jnp.sum(p, axis=-1, keepdims=True)))
    ycs = [(_dot(p, mv_ref[:, sl]) / l).astype(BF16) for (p, l), sl in zip(probs, heads)]
    proj_b = _dot(yb_ref[...], wb_ref[1])
    proj_c = _dot(jnp.concatenate(ycs, axis=1), wb_ref[2])

    merged = None
    for n, proj in enumerate((proj_a, proj_b, proj_c)):
        term = jax.nn.sigmoid(gate_ref[:, n * D_MODEL:(n + 1) * D_MODEL].astype(F32)) * proj
        merged = term if merged is None else merged + term
    o_ref[...] = x_ref[...] + _dot(merged.astype(BF16), wo_ref[...])


def _merge(x, ya, yb, mq, gates, mk, mv, wb, wo, seq):
    t = x.shape[0]
    tm = MERGE_TOKENS
    n_s = seq // tm

    def row(width):
        return pl.BlockSpec((tm, width), lambda i: (i, 0))

    memspec = pl.BlockSpec((N_MEM, MEM_WIDTH), lambda i: (i // n_s, 0))
    return pl.pallas_call(
        _merge_body,
        grid=(t // tm,),
        in_specs=[row(D_MODEL), row(D_MODEL), row(D_MODEL), row(MEM_WIDTH), row(N_BRANCH * D_MODEL),
                  memspec, memspec, _const_spec(wb.shape), _const_spec(wo.shape)],
        out_specs=row(D_MODEL),
        out_shape=jax.ShapeDtypeStruct((t, D_MODEL), F32),
        compiler_params=_params("parallel"),
        name="merge",
    )(x, ya, yb, mq, gates, mk, mv, wb, wo)


def _rotary_tables(seq):
    pos = np.arange(seq, dtype=np.float32)
    inv_freq = np.float32(1.0) / (np.float32(ROPE_THETA) ** (np.arange(0, ROPE_DIM, 2, dtype=np.float32) / np.float32(ROPE_DIM)))
    ang = pos[:, None] * inv_freq[None, :]
    cos, sin = np.cos(ang).astype(np.float32), np.sin(ang).astype(np.float32)
    zero = np.zeros_like(cos)

    def lay(a, b):
        return np.concatenate([a, b, np.zeros((seq, LANES - 2 * HALF_ROPE), np.float32)], axis=1)

    return tuple(jnp.asarray(t) for t in (lay(cos, cos), lay(-sin, zero), lay(zero, sin)))


def _pad_heads(w, keep, per_head, pad_to):
    fan_in = w.shape[0]
    w = w.reshape(fan_in, -1, per_head)[:, :, :keep]
    w = jnp.pad(w, ((0, 0), (0, 0), (0, pad_to - keep)))
    return w.reshape(fan_in, -1)


def kernel(x, mem, ffn1_norm, ffn1_w_in, ffn1_w_down, mix_norm, w_in, b_gate, q_norm, w_uq, kv_norm, w_ukv,
           conv_w, conv_b, w_rg_a, b_rg_a, w_rg_i, b_rg_i, lru_lambda, mem_norm, w_mem_kv, w_branch, w_out,
           ffn2_norm, ffn2_w_in, ffn2_w_down, final_norm):
    bsz, seq, _ = x.shape
    depth = ffn1_norm.shape[0]
    xt = x.reshape(bsz * seq, D_MODEL)
    memt = mem.reshape(bsz * N_MEM, D_MODEL)
    tables = _rotary_tables(seq)
    fn = final_norm.reshape(1, D_MODEL)
    bf = lambda a: a.astype(BF16)

    for l in range(depth):
        last = l == depth - 1
        xt = _ffn(xt, ffn1_norm[l].reshape(1, -1), bf(ffn1_w_in[l]), bf(ffn1_w_down[l]), fn, final=False)

        c0 = Q_LORA + KV_LORA + ROPE_DIM
        win = jnp.concatenate([bf(w_in[l][:, :c0]), jnp.zeros((D_MODEL, LAT_WIDTH - c0), BF16), bf(w_in[l][:, c0:])], axis=1)
        wuq = bf(_pad_heads(w_uq[l], NOPE_DIM + ROPE_DIM, NOPE_DIM + ROPE_DIM, QK_PAD))
        wukv = w_ukv[l].reshape(KV_LORA, MLA_HEADS, NOPE_DIM + V_DIM)
        wuk = bf(wukv[:, :, :NOPE_DIM].reshape(KV_LORA, -1))
        wuvt = bf(wukv[:, :, NOPE_DIM:].reshape(KV_LORA, -1).T)
        wai = bf(jnp.concatenate([w_rg_a[l], w_rg_i[l]], axis=-1))
        bai = jnp.concatenate([b_rg_a[l], b_rg_i[l]], axis=-1).reshape(RNN_BLOCKS, 1, 2 * RNN_BLOCK_DIM)
        consts = [mix_norm[l].reshape(1, -1), win, b_gate[l].reshape(1, -1),
                  q_norm[l].reshape(1, -1), wuq, kv_norm[l].reshape(1, -1), wuk, wuvt,
                  conv_w[l].reshape(CONV_WIDTH, RNN_WIDTH), conv_b[l].reshape(1, -1), wai, bai,
                  lru_lambda[l].reshape(1, -1)]
        q, k, vt, yb, mq, gates, qsq, ksq = _mix_in(xt, seq, consts, tables)

        ya = _mla_attention(q, k, vt, qsq, ksq, seq)
        mk, mv = _memkv(memt, mem_norm[l].reshape(1, -1), bf(w_mem_kv[l]))
        xt = _merge(xt, ya, yb, mq, gates, mk, mv, bf(w_branch[l]), bf(w_out[l]), seq)

        xt = _ffn(xt, ffn2_norm[l].reshape(1, -1), bf(ffn2_w_in[l]), bf(ffn2_w_down[l]), fn, final=last)
    return xt.reshape(bsz, seq, D_MODEL)
```

```python
import functools

import jax
import jax.numpy as jnp
import numpy as np
from jax import lax
from jax.experimental import pallas as pl
from jax.experimental.pallas import tpu as pltpu

F32 = jnp.float32
BF16 = jnp.bfloat16

D_MODEL = 1024
CHUNK = 64
N_MEM = 256
RMS_EPS = 1e-6
FFN_HIDDEN = 2816
MLA_HEADS = 8
Q_LORA = 384
KV_LORA = 256
NOPE_DIM = 128
ROPE_DIM = 64
V_DIM = 128
ROPE_THETA = 10000.0
RNN_WIDTH = 1024
RNN_BLOCKS = 8
RNN_BLOCK_DIM = RNN_WIDTH // RNN_BLOCKS
CONV_WIDTH = 4
LRU_C = 8.0
MEM_HEADS = 4
MEM_HEAD_DIM = 256
MEM_WIDTH = MEM_HEADS * MEM_HEAD_DIM
N_BRANCH = 3

LOG2E = float(np.log2(np.e))
Q_SCALE = (NOPE_DIM + ROPE_DIM) ** -0.5 * LOG2E
MQ_SCALE = MEM_HEAD_DIM ** -0.5 * LOG2E
HALF_ROPE = ROPE_DIM // 2
LANES = 128
QK_PAD = 256
LAT_WIDTH = 768
WIN_X = 0
WIN_G = WIN_X + RNN_WIDTH
WIN_MQ = WIN_G + RNN_WIDTH
WIN_GATE = WIN_MQ + MEM_WIDTH
HALO = 8

VMEM_LIMIT_BYTES = 56 * 1024 * 1024

FFN_TOKENS = 1024
FFN_PARTS = 4
FFN_CHUNKS = ((0, 1536), (1536, FFN_HIDDEN))
MIX_TOKENS = 512
MIX_GROUPS = 4
MIX_GROUP_WIDTH = RNN_WIDTH // MIX_GROUPS
SCAN_NSEG = 8
SCAN_SEG = MIX_TOKENS // SCAN_NSEG
SCAN_PITCH = SCAN_SEG + 8
ATTN_Q = 256
ATTN_BOUND_MARGIN = 1.05
ATTN_BOUND_LIMIT = 60.0
ATTN_SLAB = 64
MERGE_TOKENS = 512
MEMKV_TOKENS = 1024
WIN_PREP_ROWS = 128


def _dot(a, b):
    return jnp.dot(a, b, preferred_element_type=F32)


def _dot_nt(a, b):
    return lax.dot_general(a, b, (((1,), (1,)), ((), ())), preferred_element_type=F32)


def _rms(x, g):
    return x * lax.rsqrt(jnp.mean(x * x, axis=-1, keepdims=True) + RMS_EPS) * g


def _const_spec(shape):
    nd = len(shape)
    return pl.BlockSpec(shape, lambda *_: (0,) * nd, pipeline_mode=pl.Buffered(1))


def _params(*sem):
    return pltpu.CompilerParams(dimension_semantics=sem, vmem_limit_bytes=VMEM_LIMIT_BYTES)


def _ffn_body(x_ref, g_ref, wgu_ref, wd_ref, fn_ref, o_ref, *, final):
    part_rows = x_ref.shape[0] // FFN_PARTS
    for r in range(FFN_PARTS):
        rows = slice(r * part_rows, (r + 1) * part_rows)
        x = x_ref[rows, :]
        h = _rms(x, g_ref[...]).astype(BF16)
        acc = None
        for lo, hi in FFN_CHUNKS:
            gate = _dot(h, wgu_ref[:, lo:hi])
            up = _dot(h, wgu_ref[:, FFN_HIDDEN + lo:FFN_HIDDEN + hi])
            a = (gate * jax.nn.sigmoid(gate) * up).astype(BF16)
            part = _dot(a, wd_ref[lo:hi, :])
            acc = part if acc is None else acc + part
        y = x + 0.5 * acc
        if final:
            y = _rms(y, fn_ref[...])
        o_ref[rows, :] = y


def _ffn(x, g, wgu, wd, fn, *, final):
    t = x.shape[0]
    tm = FFN_TOKENS
    row = pl.BlockSpec((tm, D_MODEL), lambda i: (i, 0))
    return pl.pallas_call(
        functools.partial(_ffn_body, final=final),
        grid=(t // tm,),
        in_specs=[row, _const_spec((1, D_MODEL)), _const_spec(wgu.shape), _const_spec(wd.shape),
                  _const_spec((1, D_MODEL))],
        out_specs=row,
        out_shape=jax.ShapeDtypeStruct((t, D_MODEL), F32),
        compiler_params=_params("parallel"),
        name="ffn_final" if final else "ffn",
    )(x, g, wgu, wd, fn)


def _split_win_body(w_ref, lat_out, rest_out):
    w = w_ref[...]
    c0 = Q_LORA + KV_LORA + ROPE_DIM
    pad = jnp.zeros((w.shape[0], LAT_WIDTH - c0), F32)
    lat_out[...] = jnp.concatenate([w[:, :c0], pad], axis=1).astype(BF16)
    rest_out[...] = w[:, c0:].astype(BF16)


def _split_win(w):
    rows, width = w.shape
    c0 = Q_LORA + KV_LORA + ROPE_DIM
    tr = WIN_PREP_ROWS
    return pl.pallas_call(
        _split_win_body,
        grid=(rows // tr,),
        in_specs=[pl.BlockSpec((tr, width), lambda i: (i, 0))],
        out_specs=[pl.BlockSpec((tr, LAT_WIDTH), lambda i: (i, 0)), pl.BlockSpec((tr, width - c0), lambda i: (i, 0))],
        out_shape=[jax.ShapeDtypeStruct((rows, LAT_WIDTH), BF16), jax.ShapeDtypeStruct((rows, width - c0), BF16)],
        compiler_params=_params("parallel"),
        name="split_win",
    )(w)


def _gelu_tanh(x):
    c = float(np.sqrt(2.0 / np.pi))
    inner = x * ((x * x) * (c * 0.044715) + c)
    return x * (0.5 * jnp.tanh(inner) + 0.5)


def _rot(x, c, s_lo, s_hi):
    w = x.shape[-1]
    return x * c + pltpu.roll(x, w - HALF_ROPE, 1) * s_lo + pltpu.roll(x, HALF_ROPE, 1) * s_hi


def _rglru_coeffs(xc, gates, n0, bai_ref, neg_c_log2, pos_c, a_s, u_s):
    for k, g in enumerate(gates):
        n = n0 + k
        loc = slice(k * RNN_BLOCK_DIM, (k + 1) * RNN_BLOCK_DIM)
        blk = slice(n * RNN_BLOCK_DIM, (n + 1) * RNN_BLOCK_DIM)
        g = g + bai_ref[n]
        r = jax.nn.sigmoid(g[:, :RNN_BLOCK_DIM])
        ig = jax.nn.sigmoid(g[:, RNN_BLOCK_DIM:])
        a = jnp.exp2(r * neg_c_log2[:, blk])
        y = jnp.tanh(r * pos_c[:, blk]) * (1.0 + a * a)
        root = jnp.where(y == 0.0, 0.0, y * lax.rsqrt(y))
        u = root * (ig * xc[:, loc])
        for s in range(SCAN_NSEG):
            src = slice(s * SCAN_SEG, (s + 1) * SCAN_SEG)
            dst = slice(s * SCAN_PITCH, s * SCAN_PITCH + SCAN_SEG)
            a_s[n, dst, :] = a[src, :]
            u_s[n, dst, :] = u[src, :]


def _segmented_scan(a_s, u_s, n, h_in):
    h_loc = prod = None
    for j in range(SCAN_SEG):
        rows = pl.ds(j, SCAN_NSEG, stride=SCAN_PITCH)
        aj, uj = a_s[n, rows, :], u_s[n, rows, :]
        h_loc = uj if j == 0 else aj * h_loc + uj
        prod = aj if j == 0 else aj * prod
        u_s[n, rows, :] = h_loc
        a_s[n, rows, :] = prod
    seg_in = []
    for s in range(SCAN_NSEG):
        seg_in.append(h_in)
        last = s * SCAN_PITCH + SCAN_SEG - 1
        h_in = a_s[n, last:last + 1, :] * h_in + u_s[n, last:last + 1, :]
    return seg_in, h_in


def _mix_in_body(x_ref, gmix_ref, wlat_ref, win_ref, bgate_ref,
                 qnorm_ref, wuq_ref, kvnorm_ref, wuk_ref, wuvt_ref,
                 cw_ref, cb_ref, wai_ref, bai_ref, lam_ref,
                 rc_ref, rlo_ref, rhi_ref,
                 q_out, k_out, vt_out, yb_out, mq_out, gate_out, qsq_out, ksq_out,
                 xbuf, hstate, a_s, u_s):
    ts = MIX_TOKENS

    @pl.when(pl.program_id(1) == 0)
    def _():
        xbuf[0:HALO, :] = jnp.zeros((HALO, RNN_WIDTH), F32)
        hstate[...] = jnp.zeros((1, RNN_WIDTH), F32)

    h = _rms(x_ref[...], gmix_ref[...]).astype(BF16)
    lat = _dot(h, wlat_ref[...])
    hq = _rms(lat[:, :Q_LORA], qnorm_ref[...] * Q_SCALE).astype(BF16)
    hkv = _rms(lat[:, Q_LORA:Q_LORA + KV_LORA], kvnorm_ref[...]).astype(BF16)
    rot_c, rot_lo, rot_hi = rc_ref[...], rlo_ref[...], rhi_ref[...]
    kr = _rot(lat[:, Q_LORA + KV_LORA:], rot_c, rot_lo, rot_hi)
    kr_sq = jnp.sum(kr * kr, axis=1, keepdims=True)
    kr = kr.astype(BF16)

    lam = lam_ref[...]
    sp = jnp.maximum(-lam, 0.0) + jnp.log1p(jnp.exp(-jnp.abs(lam)))
    neg_c_log2 = (-LRU_C * LOG2E) * sp
    pos_c = LRU_C * sp

    blocks = MIX_GROUP_WIDTH // RNN_BLOCK_DIM
    heads = MIX_GROUP_WIDTH // NOPE_DIM
    for grp in range(MIX_GROUPS):
        c0, c1 = grp * MIX_GROUP_WIDTH, (grp + 1) * MIX_GROUP_WIDTH
        cols = slice(c0, c1)
        gcols = slice(grp * N_BRANCH * MIX_GROUP_WIDTH, (grp + 1) * N_BRANCH * MIX_GROUP_WIDTH)
        n0 = grp * blocks

        xbuf[HALO:HALO + ts, cols] = _dot(h, win_ref[:, WIN_X + c0:WIN_X + c1])
        q = _dot(hq, wuq_ref[:, grp * heads * QK_PAD:(grp + 1) * heads * QK_PAD])
        kn = _dot(hkv, wuk_ref[:, cols])
        vt_out[cols, :] = _dot_nt(wuvt_ref[cols, :], hkv).astype(BF16)

        xc = cb_ref[:, cols] + cw_ref[CONV_WIDTH - 1:CONV_WIDTH, cols] * xbuf[HALO:HALO + ts, cols]
        for d in range(1, CONV_WIDTH):
            xc = xc + cw_ref[CONV_WIDTH - 1 - d:CONV_WIDTH - d, cols] * xbuf[HALO - d:HALO - d + ts, cols]
        xbuf[0:HALO, cols] = xbuf[ts:ts + HALO, cols]
        xcb = xc.astype(BF16)
        gates = [_dot(xcb[:, k * RNN_BLOCK_DIM:(k + 1) * RNN_BLOCK_DIM], wai_ref[n0 + k]) for k in range(blocks)]

        mq = _dot(h, win_ref[:, WIN_MQ + c0:WIN_MQ + c1])
        gz = _dot(h, win_ref[:, WIN_G + c0:WIN_G + c1])
        merge_gate = _dot(h, win_ref[:, WIN_GATE + N_BRANCH * c0:WIN_GATE + N_BRANCH * c1])

        _rglru_coeffs(xc, gates, n0, bai_ref, neg_c_log2, pos_c, a_s, u_s)
        gz = _gelu_tanh(gz)
        for k in range(blocks):
            n = n0 + k
            blk = slice(n * RNN_BLOCK_DIM, (n + 1) * RNN_BLOCK_DIM)
            seg_in, h_last = _segmented_scan(a_s, u_s, n, hstate[:, blk])
            hstate[:, blk] = h_last
            for s, h_seg in enumerate(seg_in):
                rows = slice(s * SCAN_SEG, (s + 1) * SCAN_SEG)
                pad = slice(s * SCAN_PITCH, s * SCAN_PITCH + SCAN_SEG)
                h_true = u_s[n, pad, :] + a_s[n, pad, :] * h_seg
                yb_out[rows, blk] = (h_true * gz[rows, k * RNN_BLOCK_DIM:(k + 1) * RNN_BLOCK_DIM]).astype(BF16)

        for k in range(heads):
            hd = grp * heads + k
            base = hd * QK_PAD
            q_nope = q[:, k * QK_PAD:k * QK_PAD + NOPE_DIM]
            q_rope = _rot(q[:, k * QK_PAD + NOPE_DIM:(k + 1) * QK_PAD], rot_c, rot_lo, rot_hi)
            k_nope = kn[:, k * NOPE_DIM:(k + 1) * NOPE_DIM]
            q_out[:, base:base + NOPE_DIM] = q_nope.astype(BF16)
            q_out[:, base + NOPE_DIM:base + QK_PAD] = q_rope.astype(BF16)
            k_out[:, base:base + NOPE_DIM] = k_nope.astype(BF16)
            k_out[:, base + NOPE_DIM:base + QK_PAD] = kr
            q_sq = jnp.sum(q_nope * q_nope + q_rope * q_rope, axis=1, keepdims=True)
            k_sq = jnp.sum(k_nope * k_nope, axis=1, keepdims=True) + kr_sq
            qsq_out[0, 0, hd:hd + 1, :] = jnp.broadcast_to(jnp.max(q_sq, axis=0, keepdims=True), (1, LANES))
            ksq_out[0, 0, hd:hd + 1, :] = jnp.broadcast_to(jnp.max(k_sq, axis=0, keepdims=True), (1, LANES))
        mq_out[:, cols] = mq.astype(BF16)
        gate_out[:, gcols] = (merge_gate + bgate_ref[:, gcols]).astype(BF16)


def _mix_in(x, seq, consts, tables):
    t = x.shape[0]
    tm = MIX_TOKENS
    n_s = seq // tm
    bsz = t // seq

    def row(width):
        return pl.BlockSpec((tm, width), lambda b, i: (b * n_s + i, 0))

    def tab(width):
        return pl.BlockSpec((tm, width), lambda b, i: (i, 0))

    row_outs = [(MLA_HEADS * QK_PAD, BF16), (MLA_HEADS * QK_PAD, BF16)]
    tail_outs = [(RNN_WIDTH, BF16), (MEM_WIDTH, BF16), (N_BRANCH * D_MODEL, BF16)]
    vt_spec = pl.BlockSpec((MLA_HEADS * V_DIM, tm), lambda b, i: (b, i))
    vt_shape = jax.ShapeDtypeStruct((bsz * MLA_HEADS * V_DIM, seq), BF16)
    sq_spec = pl.BlockSpec((1, 1, MLA_HEADS, LANES), lambda b, i: (b, i, 0, 0))
    sq_shape = jax.ShapeDtypeStruct((bsz, n_s, MLA_HEADS, LANES), F32)
    return pl.pallas_call(
        _mix_in_body,
        grid=(bsz, n_s),
        in_specs=[row(D_MODEL)] + [_const_spec(c.shape) for c in consts] + [tab(LANES)] * 3,
        out_specs=[row(w) for w, _ in row_outs] + [vt_spec] + [row(w) for w, _ in tail_outs] + [sq_spec] * 2,
        out_shape=[jax.ShapeDtypeStruct((t, w), dt) for w, dt in row_outs] + [vt_shape]
                  + [jax.ShapeDtypeStruct((t, w), dt) for w, dt in tail_outs] + [sq_shape] * 2,
        scratch_shapes=[pltpu.VMEM((HALO + tm, RNN_WIDTH), F32), pltpu.VMEM((1, RNN_WIDTH), F32),
                        pltpu.VMEM((RNN_BLOCKS, SCAN_NSEG * SCAN_PITCH, RNN_BLOCK_DIM), F32),
                        pltpu.VMEM((RNN_BLOCKS, SCAN_NSEG * SCAN_PITCH, RNN_BLOCK_DIM), F32)],
        compiler_params=_params("parallel", "arbitrary"),
        name="mix_in",
    )(x, *consts, *tables)


def _col_reduce(op, x):
    n, w = x.shape
    return op(op(x.reshape(n // ATTN_SLAB, ATTN_SLAB, w), axis=0), axis=0, keepdims=True)


def _attn_body(q_ref, k_ref, vt_ref, qsq_ref, ksq_ref, o_ref, s_buf, p_buf):
    seq = q_ref.shape[0]
    tq = ATTN_Q
    key_chunk = lax.broadcasted_iota(jnp.int32, (tq, tq), 0) // CHUNK
    qry_chunk = lax.broadcasted_iota(jnp.int32, (tq, tq), 1) // CHUNK
    diag_mask = key_chunk <= qry_chunk
    order = list(reversed(range(seq // tq)))

    def finish(i, l):
        lo, hi = i * tq, (i + 1) * tq
        acc = _dot(vt_ref[:, 0:hi], p_buf[i % 2, 0:hi, :])
        o_ref[lo:hi, :] = (acc / l).T.astype(BF16)

    head_row = lax.broadcasted_iota(jnp.int32, (MLA_HEADS, LANES), 0) == pl.program_id(1)

    def head_max(ref):
        vals = jnp.max(ref[0], axis=0)
        return jnp.max(jnp.where(head_row, vals, 0.0), axis=0, keepdims=True)

    bound = jnp.sqrt(head_max(qsq_ref) * head_max(ksq_ref)) * ATTN_BOUND_MARGIN
    small = bound[0, 0] <= ATTN_BOUND_LIMIT

    @pl.when(small)
    def _():
        shift = jnp.concatenate([bound] * (tq // LANES), axis=1)
        blocks = [(i, kb) for i in order for kb in range(i + 1)]

        def score(i, kb):
            st = _dot_nt(k_ref[kb * tq:(kb + 1) * tq, :], q_ref[i * tq:(i + 1) * tq, :])
            return jnp.where(diag_mask, st, -jnp.inf) if kb == i else st

        st = score(*blocks[0])
        l = None
        for n, (i, kb) in enumerate(blocks):
            st_next = score(*blocks[n + 1]) if n + 1 < len(blocks) else None
            pt = jnp.exp2(st - shift)
            part = _col_reduce(jnp.sum, pt)
            l = part if kb == 0 else l + part
            p_buf[i % 2, kb * tq:(kb + 1) * tq, :] = pt.astype(BF16)
            if kb == i:
                finish(i, l)
            st = st_next

    @pl.when(jnp.logical_not(small))
    def _():
        def scores(i):
            lo, hi = i * tq, (i + 1) * tq
            q = q_ref[lo:hi, :]
            s_buf[i % 2, lo:hi, :] = jnp.where(diag_mask, _dot_nt(k_ref[lo:hi, :], q), -jnp.inf)
            for kb in range(i):
                s_buf[i % 2, kb * tq:(kb + 1) * tq, :] = _dot_nt(k_ref[kb * tq:(kb + 1) * tq, :], q)

        def softmax_pv(i):
            hi = (i + 1) * tq
            m = _col_reduce(jnp.max, s_buf[i % 2, 0:hi, :])
            pt = jnp.exp2(s_buf[i % 2, 0:hi, :] - m)
            l = _col_reduce(jnp.sum, pt)
            p_buf[i % 2, 0:hi, :] = pt.astype(BF16)
            finish(i, l)

        scores(order[0])
        for pos, i in enumerate(order):
            if pos + 1 < len(order):
                scores(order[pos + 1])
            softmax_pv(i)


def _mla_attention(q, k, vt, qsq, ksq, seq):
    t = q.shape[0]
    bsz = t // seq
    sq_spec = pl.BlockSpec((1,) + qsq.shape[1:], lambda b, h: (b, 0, 0, 0))
    return pl.pallas_call(
        _attn_body,
        grid=(bsz, MLA_HEADS),
        in_specs=[pl.BlockSpec((seq, QK_PAD), lambda b, h: (b, h)),
                  pl.BlockSpec((seq, QK_PAD), lambda b, h: (b, h)),
                  pl.BlockSpec((V_DIM, seq), lambda b, h: (b * MLA_HEADS + h, 0)),
                  sq_spec, sq_spec],
        out_specs=pl.BlockSpec((seq, V_DIM), lambda b, h: (b, h)),
        out_shape=jax.ShapeDtypeStruct((t, MLA_HEADS * V_DIM), BF16),
        scratch_shapes=[pltpu.VMEM((2, seq, ATTN_Q), F32), pltpu.VMEM((2, seq, ATTN_Q), BF16)],
        compiler_params=_params("parallel", "parallel"),
        name="mla_attn",
    )(q, k, vt, qsq, ksq)


def _memkv_body(m_ref, g_ref, wkv_ref, k_out, v_out):
    h = _rms(m_ref[...], g_ref[...]).astype(BF16)
    k_out[...] = _dot(h, wkv_ref[:, :MEM_WIDTH]).astype(BF16)
    v_out[...] = _dot(h, wkv_ref[:, MEM_WIDTH:]).astype(BF16)


def _memkv(mem, g, wkv):
    t = mem.shape[0]
    tm = MEMKV_TOKENS
    row = pl.BlockSpec((tm, D_MODEL), lambda b: (b, 0))
    out = pl.BlockSpec((tm, MEM_WIDTH), lambda b: (b, 0))
    return pl.pallas_call(
        _memkv_body,
        grid=(t // tm,),
        in_specs=[row, _const_spec(g.shape), _const_spec(wkv.shape)],
        out_specs=[out, out],
        out_shape=[jax.ShapeDtypeStruct((t, MEM_WIDTH), BF16)] * 2,
        compiler_params=_params("parallel"),
        name="memkv",
    )(mem, g, wkv)


def _merge_body(x_ref, ya_ref, yb_ref, mq_ref, gate_ref, mk_ref, mv_ref, wb_ref, wo_ref, o_ref):
    heads = [slice(hd * MEM_HEAD_DIM, (hd + 1) * MEM_HEAD_DIM) for hd in range(MEM_HEADS)]
    scores = [_dot_nt(mq_ref[:, sl], mk_ref[:, sl]) * MQ_SCALE for sl in heads]
    proj_a = _dot(ya_ref[...], wb_ref[0])
    probs = []
    for s in scores:
        p = jnp.exp2(s - jnp.max(s, axis=-1, keepdims=True))
        probs.append((p.astype(BF16), jnp.sum(p, axis=-1, keepdims=True)))
    ycs = [(_dot(p, mv_ref[:, sl]) / l).astype(BF16) for (p, l), sl in zip(probs, heads)]
    proj_b = _dot(yb_ref[...], wb_ref[1])
    proj_c = _dot(jnp.concatenate(ycs, axis=1), wb_ref[2])

    merged = None
    for n, proj in enumerate((proj_a, proj_b, proj_c)):
        term = jax.nn.sigmoid(gate_ref[:, n * D_MODEL:(n + 1) * D_MODEL].astype(F32)) * proj
        merged = term if merged is None else merged + term
    o_ref[...] = x_ref[...] + _dot(merged.astype(BF16), wo_ref[...])


def _merge(x, ya, yb, mq, gates, mk, mv, wb, wo, seq):
    t = x.shape[0]
    tm = MERGE_TOKENS
    n_s = seq // tm

    def row(width):
        return pl.BlockSpec((tm, width), lambda i: (i, 0))

    memspec = pl.BlockSpec((N_MEM, MEM_WIDTH), lambda i: (i // n_s, 0))
    return pl.pallas_call(
        _merge_body,
        grid=(t // tm,),
        in_specs=[row(D_MODEL), row(D_MODEL), row(D_MODEL), row(MEM_WIDTH), row(N_BRANCH * D_MODEL),
                  memspec, memspec, _const_spec(wb.shape), _const_spec(wo.shape)],
        out_specs=row(D_MODEL),
        out_shape=jax.ShapeDtypeStruct((t, D_MODEL), F32),
        compiler_params=_params("parallel"),
        name="merge",
    )(x, ya, yb, mq, gates, mk, mv, wb, wo)


def _rotary_tables(seq):
    pos = np.arange(seq, dtype=np.float32)
    inv_freq = np.float32(1.0) / (np.float32(ROPE_THETA) ** (np.arange(0, ROPE_DIM, 2, dtype=np.float32) / np.float32(ROPE_DIM)))
    ang = pos[:, None] * inv_freq[None, :]
    cos, sin = np.cos(ang).astype(np.float32), np.sin(ang).astype(np.float32)
    zero = np.zeros_like(cos)

    def lay(a, b):
        return np.concatenate([a, b, np.zeros((seq, LANES - 2 * HALF_ROPE), np.float32)], axis=1)

    return tuple(jnp.asarray(t) for t in (lay(cos, cos), lay(-sin, zero), lay(zero, sin)))


def _pad_heads(w, keep, per_head, pad_to):
    fan_in = w.shape[0]
    w = w.reshape(fan_in, -1, per_head)[:, :, :keep]
    w = jnp.pad(w, ((0, 0), (0, 0), (0, pad_to - keep)))
    return w.reshape(fan_in, -1)


def kernel(x, mem, ffn1_norm, ffn1_w_in, ffn1_w_down, mix_norm, w_in, b_gate, q_norm, w_uq, kv_norm, w_ukv,
           conv_w, conv_b, w_rg_a, b_rg_a, w_rg_i, b_rg_i, lru_lambda, mem_norm, w_mem_kv, w_branch, w_out,
           ffn2_norm, ffn2_w_in, ffn2_w_down, final_norm):
    bsz, seq, _ = x.shape
    depth = ffn1_norm.shape[0]
    xt = x.reshape(bsz * seq, D_MODEL)
    memt = mem.reshape(bsz * N_MEM, D_MODEL)
    tables = _rotary_tables(seq)
    fn = final_norm.reshape(1, D_MODEL)
    bf = lambda a: a.astype(BF16)

    for l in range(depth):
        last = l == depth - 1
        xt = _ffn(xt, ffn1_norm[l].reshape(1, -1), bf(ffn1_w_in[l]), bf(ffn1_w_down[l]), fn, final=False)

        wlat, win = _split_win(w_in[l])
        wuq = bf(_pad_heads(w_uq[l], NOPE_DIM + ROPE_DIM, NOPE_DIM + ROPE_DIM, QK_PAD))
        wukv = w_ukv[l].reshape(KV_LORA, MLA_HEADS, NOPE_DIM + V_DIM)
        wuk = bf(wukv[:, :, :NOPE_DIM].reshape(KV_LORA, -1))
        wuvt = bf(wukv[:, :, NOPE_DIM:].reshape(KV_LORA, -1).T)
        wai = bf(jnp.concatenate([w_rg_a[l], w_rg_i[l]], axis=-1))
        bai = jnp.concatenate([b_rg_a[l], b_rg_i[l]], axis=-1).reshape(RNN_BLOCKS, 1, 2 * RNN_BLOCK_DIM)
        consts = [mix_norm[l].reshape(1, -1), wlat, win, b_gate[l].reshape(1, -1),
                  q_norm[l].reshape(1, -1), wuq, kv_norm[l].reshape(1, -1), wuk, wuvt,
                  conv_w[l].reshape(CONV_WIDTH, RNN_WIDTH), conv_b[l].reshape(1, -1), wai, bai,
                  lru_lambda[l].reshape(1, -1)]
        q, k, vt, yb, mq, gates, qsq, ksq = _mix_in(xt, seq, consts, tables)

        ya = _mla_attention(q, k, vt, qsq, ksq, seq)
        mk, mv = _memkv(memt, mem_norm[l].reshape(1, -1), bf(w_mem_kv[l]))
        xt = _merge(xt, ya, yb, mq, gates, mk, mv, bf(w_branch[l]), bf(w_out[l]), seq)

        xt = _ffn(xt, ffn2_norm[l].reshape(1, -1), bf(ffn2_w_in[l]), bf(ffn2_w_down[l]), fn, final=last)
    return xt.reshape(bsz, seq, D_MODEL)
```

```python
import functools

import jax
import jax.numpy as jnp
import numpy as np
from jax import lax
from jax.experimental import pallas as pl
from jax.experimental.pallas import tpu as pltpu

F32 = jnp.float32
BF16 = jnp.bfloat16

D_MODEL = 1024
CHUNK = 64
N_MEM = 256
RMS_EPS = 1e-6
FFN_HIDDEN = 2816
MLA_HEADS = 8
Q_LORA = 384
KV_LORA = 256
NOPE_DIM = 128
ROPE_DIM = 64
V_DIM = 128
ROPE_THETA = 10000.0
RNN_WIDTH = 1024
RNN_BLOCKS = 8
RNN_BLOCK_DIM = RNN_WIDTH // RNN_BLOCKS
CONV_WIDTH = 4
LRU_C = 8.0
MEM_HEADS = 4
MEM_HEAD_DIM = 256
MEM_WIDTH = MEM_HEADS * MEM_HEAD_DIM
N_BRANCH = 3

LOG2E = float(np.log2(np.e))
Q_SCALE = (NOPE_DIM + ROPE_DIM) ** -0.5 * LOG2E
MQ_SCALE = MEM_HEAD_DIM ** -0.5 * LOG2E
HALF_ROPE = ROPE_DIM // 2
LANES = 128
QK_PAD = 256
LAT_WIDTH = 768
WIN_X = Q_LORA + KV_LORA + ROPE_DIM
WIN_G = WIN_X + RNN_WIDTH
WIN_MQ = WIN_G + RNN_WIDTH
WIN_GATE = WIN_MQ + MEM_WIDTH
HALO = 8

VMEM_LIMIT_BYTES = 56 * 1024 * 1024

FFN_TOKENS = 1024
FFN_PARTS = 4
FFN_CHUNKS = ((0, 1536), (1536, FFN_HIDDEN))
MIX_TOKENS = 512
MIX_GROUPS = 4
MIX_GROUP_WIDTH = RNN_WIDTH // MIX_GROUPS
SCAN_NSEG = 8
SCAN_SEG = MIX_TOKENS // SCAN_NSEG
SCAN_PITCH = SCAN_SEG + 8
ATTN_Q = 256
ATTN_BOUND_MARGIN = 1.05
ATTN_BOUND_LIMIT = 60.0
ATTN_SLAB = 64
MERGE_TOKENS = 512
MEMKV_TOKENS = 1024


def _dot(a, b):
    return jnp.dot(a, b, preferred_element_type=F32)


def _dot_nt(a, b):
    return lax.dot_general(a, b, (((1,), (1,)), ((), ())), preferred_element_type=F32)


def _rms(x, g):
    return x * lax.rsqrt(jnp.mean(x * x, axis=-1, keepdims=True) + RMS_EPS) * g


def _const_spec(shape):
    nd = len(shape)
    return pl.BlockSpec(shape, lambda *_: (0,) * nd, pipeline_mode=pl.Buffered(1))


def _params(*sem):
    return pltpu.CompilerParams(dimension_semantics=sem, vmem_limit_bytes=VMEM_LIMIT_BYTES)


def _ffn_body(x_ref, g_ref, wgu_ref, wd_ref, fn_ref, o_ref, *, final):
    part_rows = x_ref.shape[0] // FFN_PARTS
    for r in range(FFN_PARTS):
        rows = slice(r * part_rows, (r + 1) * part_rows)
        x = x_ref[rows, :]
        h = _rms(x, g_ref[...]).astype(BF16)
        acc = None
        for lo, hi in FFN_CHUNKS:
            gate = _dot(h, wgu_ref[:, lo:hi])
            up = _dot(h, wgu_ref[:, FFN_HIDDEN + lo:FFN_HIDDEN + hi])
            a = (gate * jax.nn.sigmoid(gate) * up).astype(BF16)
            part = _dot(a, wd_ref[lo:hi, :])
            acc = part if acc is None else acc + part
        y = x + 0.5 * acc
        if final:
            y = _rms(y, fn_ref[...])
        o_ref[rows, :] = y


def _ffn(x, g, wgu, wd, fn, *, final):
    t = x.shape[0]
    tm = FFN_TOKENS
    row = pl.BlockSpec((tm, D_MODEL), lambda i: (i, 0))
    return pl.pallas_call(
        functools.partial(_ffn_body, final=final),
        grid=(t // tm,),
        in_specs=[row, _const_spec((1, D_MODEL)), _const_spec(wgu.shape), _const_spec(wd.shape),
                  _const_spec((1, D_MODEL))],
        out_specs=row,
        out_shape=jax.ShapeDtypeStruct((t, D_MODEL), F32),
        compiler_params=_params("parallel"),
        name="ffn_final" if final else "ffn",
    )(x, g, wgu, wd, fn)


def _gelu_tanh(x):
    c = float(np.sqrt(2.0 / np.pi))
    inner = x * ((x * x) * (c * 0.044715) + c)
    return x * (0.5 * jnp.tanh(inner) + 0.5)


def _rot(x, c, s_lo, s_hi):
    w = x.shape[-1]
    return x * c + pltpu.roll(x, w - HALF_ROPE, 1) * s_lo + pltpu.roll(x, HALF_ROPE, 1) * s_hi


def _rglru_coeffs(xc, gates, n0, bai_ref, neg_c_log2, pos_c, a_s, u_s):
    for k, g in enumerate(gates):
        n = n0 + k
        loc = slice(k * RNN_BLOCK_DIM, (k + 1) * RNN_BLOCK_DIM)
        blk = slice(n * RNN_BLOCK_DIM, (n + 1) * RNN_BLOCK_DIM)
        g = g + bai_ref[n]
        r = jax.nn.sigmoid(g[:, :RNN_BLOCK_DIM])
        ig = jax.nn.sigmoid(g[:, RNN_BLOCK_DIM:])
        a = jnp.exp2(r * neg_c_log2[:, blk])
        y = jnp.tanh(r * pos_c[:, blk]) * (1.0 + a * a)
        root = jnp.where(y == 0.0, 0.0, y * lax.rsqrt(y))
        u = root * (ig * xc[:, loc])
        for s in range(SCAN_NSEG):
            src = slice(s * SCAN_SEG, (s + 1) * SCAN_SEG)
            dst = slice(s * SCAN_PITCH, s * SCAN_PITCH + SCAN_SEG)
            a_s[n, dst, :] = a[src, :]
            u_s[n, dst, :] = u[src, :]


def _segmented_scan(a_s, u_s, n, h_in):
    h_loc = prod = None
    for j in range(SCAN_SEG):
        rows = pl.ds(j, SCAN_NSEG, stride=SCAN_PITCH)
        aj, uj = a_s[n, rows, :], u_s[n, rows, :]
        h_loc = uj if j == 0 else aj * h_loc + uj
        prod = aj if j == 0 else aj * prod
        u_s[n, rows, :] = h_loc
        a_s[n, rows, :] = prod
    seg_in = []
    for s in range(SCAN_NSEG):
        seg_in.append(h_in)
        last = s * SCAN_PITCH + SCAN_SEG - 1
        h_in = a_s[n, last:last + 1, :] * h_in + u_s[n, last:last + 1, :]
    return seg_in, h_in


def _mix_in_body(x_ref, gmix_ref, win_ref, bgate_ref,
                 qnorm_ref, wuq_ref, kvnorm_ref, wuk_ref, wuvt_ref,
                 cw_ref, cb_ref, wai_ref, bai_ref, lam_ref,
                 rc_ref, rlo_ref, rhi_ref,
                 q_out, k_out, vt_out, yb_out, mq_out, gate_out, qsq_out, ksq_out,
                 xbuf, hstate, a_s, u_s):
    ts = MIX_TOKENS

    @pl.when(pl.program_id(1) == 0)
    def _():
        xbuf[0:HALO, :] = jnp.zeros((HALO, RNN_WIDTH), F32)
        hstate[...] = jnp.zeros((1, RNN_WIDTH), F32)

    h = _rms(x_ref[...], gmix_ref[...]).astype(BF16)
    lat = _dot_nt(h, win_ref[0:LAT_WIDTH, :])
    hq = _rms(lat[:, :Q_LORA], qnorm_ref[...] * Q_SCALE).astype(BF16)
    hkv = _rms(lat[:, Q_LORA:Q_LORA + KV_LORA], kvnorm_ref[...]).astype(BF16)
    rot_c, rot_lo, rot_hi = rc_ref[...], rlo_ref[...], rhi_ref[...]
    kr = _rot(lat[:, Q_LORA + KV_LORA:], rot_c, rot_lo, rot_hi)
    kr_sq = jnp.sum(kr * kr, axis=1, keepdims=True)
    kr = kr.astype(BF16)

    lam = lam_ref[...]
    sp = jnp.maximum(-lam, 0.0) + jnp.log1p(jnp.exp(-jnp.abs(lam)))
    neg_c_log2 = (-LRU_C * LOG2E) * sp
    pos_c = LRU_C * sp

    blocks = MIX_GROUP_WIDTH // RNN_BLOCK_DIM
    heads = MIX_GROUP_WIDTH // NOPE_DIM
    for grp in range(MIX_GROUPS):
        c0, c1 = grp * MIX_GROUP_WIDTH, (grp + 1) * MIX_GROUP_WIDTH
        cols = slice(c0, c1)
        gcols = slice(grp * N_BRANCH * MIX_GROUP_WIDTH, (grp + 1) * N_BRANCH * MIX_GROUP_WIDTH)
        n0 = grp * blocks

        xbuf[HALO:HALO + ts, cols] = _dot_nt(h, win_ref[WIN_X + c0:WIN_X + c1, :])
        q = _dot(hq, wuq_ref[:, grp * heads * QK_PAD:(grp + 1) * heads * QK_PAD])
        kn = _dot(hkv, wuk_ref[:, cols])
        vt_out[cols, :] = _dot_nt(wuvt_ref[cols, :], hkv).astype(BF16)

        xc = cb_ref[:, cols] + cw_ref[CONV_WIDTH - 1:CONV_WIDTH, cols] * xbuf[HALO:HALO + ts, cols]
        for d in range(1, CONV_WIDTH):
            xc = xc + cw_ref[CONV_WIDTH - 1 - d:CONV_WIDTH - d, cols] * xbuf[HALO - d:HALO - d + ts, cols]
        xbuf[0:HALO, cols] = xbuf[ts:ts + HALO, cols]
        xcb = xc.astype(BF16)
        gates = [_dot(xcb[:, k * RNN_BLOCK_DIM:(k + 1) * RNN_BLOCK_DIM], wai_ref[n0 + k]) for k in range(blocks)]

        mq = _dot_nt(h, win_ref[WIN_MQ + c0:WIN_MQ + c1, :])
        gz = _dot_nt(h, win_ref[WIN_G + c0:WIN_G + c1, :])
        merge_gate = _dot_nt(h, win_ref[WIN_GATE + N_BRANCH * c0:WIN_GATE + N_BRANCH * c1, :])

        _rglru_coeffs(xc, gates, n0, bai_ref, neg_c_log2, pos_c, a_s, u_s)
        gz = _gelu_tanh(gz)
        for k in range(blocks):
            n = n0 + k
            blk = slice(n * RNN_BLOCK_DIM, (n + 1) * RNN_BLOCK_DIM)
            seg_in, h_last = _segmented_scan(a_s, u_s, n, hstate[:, blk])
            hstate[:, blk] = h_last
            for s, h_seg in enumerate(seg_in):
                rows = slice(s * SCAN_SEG, (s + 1) * SCAN_SEG)
                pad = slice(s * SCAN_PITCH, s * SCAN_PITCH + SCAN_SEG)
                h_true = u_s[n, pad, :] + a_s[n, pad, :] * h_seg
                yb_out[rows, blk] = (h_true * gz[rows, k * RNN_BLOCK_DIM:(k + 1) * RNN_BLOCK_DIM]).astype(BF16)

        for k in range(heads):
            hd = grp * heads + k
            base = hd * QK_PAD
            q_nope = q[:, k * QK_PAD:k * QK_PAD + NOPE_DIM]
            q_rope = _rot(q[:, k * QK_PAD + NOPE_DIM:(k + 1) * QK_PAD], rot_c, rot_lo, rot_hi)
            k_nope = kn[:, k * NOPE_DIM:(k + 1) * NOPE_DIM]
            q_out[:, base:base + NOPE_DIM] = q_nope.astype(BF16)
            q_out[:, base + NOPE_DIM:base + QK_PAD] = q_rope.astype(BF16)
            k_out[:, base:base + NOPE_DIM] = k_nope.astype(BF16)
            k_out[:, base + NOPE_DIM:base + QK_PAD] = kr
            q_sq = jnp.sum(q_nope * q_nope + q_rope * q_rope, axis=1, keepdims=True)
            k_sq = jnp.sum(k_nope * k_nope, axis=1, keepdims=True) + kr_sq
            qsq_out[0, 0, hd:hd + 1, :] = jnp.broadcast_to(jnp.max(q_sq, axis=0, keepdims=True), (1, LANES))
            ksq_out[0, 0, hd:hd + 1, :] = jnp.broadcast_to(jnp.max(k_sq, axis=0, keepdims=True), (1, LANES))
        mq_out[:, cols] = mq.astype(BF16)
        gate_out[:, gcols] = (merge_gate + bgate_ref[:, gcols]).astype(BF16)


def _mix_in(x, seq, consts, tables):
    t = x.shape[0]
    tm = MIX_TOKENS
    n_s = seq // tm
    bsz = t // seq

    def row(width):
        return pl.BlockSpec((tm, width), lambda b, i: (b * n_s + i, 0))

    def tab(width):
        return pl.BlockSpec((tm, width), lambda b, i: (i, 0))

    row_outs = [(MLA_HEADS * QK_PAD, BF16), (MLA_HEADS * QK_PAD, BF16)]
    tail_outs = [(RNN_WIDTH, BF16), (MEM_WIDTH, BF16), (N_BRANCH * D_MODEL, BF16)]
    vt_spec = pl.BlockSpec((MLA_HEADS * V_DIM, tm), lambda b, i: (b, i))
    vt_shape = jax.ShapeDtypeStruct((bsz * MLA_HEADS * V_DIM, seq), BF16)
    sq_spec = pl.BlockSpec((1, 1, MLA_HEADS, LANES), lambda b, i: (b, i, 0, 0))
    sq_shape = jax.ShapeDtypeStruct((bsz, n_s, MLA_HEADS, LANES), F32)
    return pl.pallas_call(
        _mix_in_body,
        grid=(bsz, n_s),
        in_specs=[row(D_MODEL)] + [_const_spec(c.shape) for c in consts] + [tab(LANES)] * 3,
        out_specs=[row(w) for w, _ in row_outs] + [vt_spec] + [row(w) for w, _ in tail_outs] + [sq_spec] * 2,
        out_shape=[jax.ShapeDtypeStruct((t, w), dt) for w, dt in row_outs] + [vt_shape]
                  + [jax.ShapeDtypeStruct((t, w), dt) for w, dt in tail_outs] + [sq_shape] * 2,
        scratch_shapes=[pltpu.VMEM((HALO + tm, RNN_WIDTH), F32), pltpu.VMEM((1, RNN_WIDTH), F32),
                        pltpu.VMEM((RNN_BLOCKS, SCAN_NSEG * SCAN_PITCH, RNN_BLOCK_DIM), F32),
                        pltpu.VMEM((RNN_BLOCKS, SCAN_NSEG * SCAN_PITCH, RNN_BLOCK_DIM), F32)],
        compiler_params=_params("parallel", "arbitrary"),
        name="mix_in",
    )(x, *consts, *tables)


def _col_reduce(op, x):
    n, w = x.shape
    return op(op(x.reshape(n // ATTN_SLAB, ATTN_SLAB, w), axis=0), axis=0, keepdims=True)


def _attn_body(q_ref, k_ref, vt_ref, qsq_ref, ksq_ref, o_ref, s_buf, p_buf):
    seq = q_ref.shape[0]
    tq = ATTN_Q
    key_chunk = lax.broadcasted_iota(jnp.int32, (tq, tq), 0) // CHUNK
    qry_chunk = lax.broadcasted_iota(jnp.int32, (tq, tq), 1) // CHUNK
    diag_mask = key_chunk <= qry_chunk
    order = list(reversed(range(seq // tq)))

    def finish(i, l):
        lo, hi = i * tq, (i + 1) * tq
        acc = _dot(vt_ref[:, 0:hi], p_buf[i % 2, 0:hi, :])
        o_ref[lo:hi, :] = (acc / l).T.astype(BF16)

    head_row = lax.broadcasted_iota(jnp.int32, (MLA_HEADS, LANES), 0) == pl.program_id(1)

    def head_max(ref):
        vals = jnp.max(ref[0], axis=0)
        return jnp.max(jnp.where(head_row, vals, 0.0), axis=0, keepdims=True)

    bound = jnp.sqrt(head_max(qsq_ref) * head_max(ksq_ref)) * ATTN_BOUND_MARGIN
    small = bound[0, 0] <= ATTN_BOUND_LIMIT

    @pl.when(small)
    def _():
        shift = jnp.concatenate([bound] * (tq // LANES), axis=1)
        blocks = [(i, kb) for i in order for kb in range(i + 1)]

        def score(i, kb):
            st = _dot_nt(k_ref[kb * tq:(kb + 1) * tq, :], q_ref[i * tq:(i + 1) * tq, :])
            return jnp.where(diag_mask, st, -jnp.inf) if kb == i else st

        st = score(*blocks[0])
        l = None
        for n, (i, kb) in enumerate(blocks):
            st_next = score(*blocks[n + 1]) if n + 1 < len(blocks) else None
            pt = jnp.exp2(st - shift)
            part = _col_reduce(jnp.sum, pt)
            l = part if kb == 0 else l + part
            p_buf[i % 2, kb * tq:(kb + 1) * tq, :] = pt.astype(BF16)
            if kb == i:
                finish(i, l)
            st = st_next

    @pl.when(jnp.logical_not(small))
    def _():
        def scores(i):
            lo, hi = i * tq, (i + 1) * tq
            q = q_ref[lo:hi, :]
            s_buf[i % 2, lo:hi, :] = jnp.where(diag_mask, _dot_nt(k_ref[lo:hi, :], q), -jnp.inf)
            for kb in range(i):
                s_buf[i % 2, kb * tq:(kb + 1) * tq, :] = _dot_nt(k_ref[kb * tq:(kb + 1) * tq, :], q)

        def softmax_pv(i):
            hi = (i + 1) * tq
            m = _col_reduce(jnp.max, s_buf[i % 2, 0:hi, :])
            pt = jnp.exp2(s_buf[i % 2, 0:hi, :] - m)
            l = _col_reduce(jnp.sum, pt)
            p_buf[i % 2, 0:hi, :] = pt.astype(BF16)
            finish(i, l)

        scores(order[0])
        for pos, i in enumerate(order):
            if pos + 1 < len(order):
                scores(order[pos + 1])
            softmax_pv(i)


def _mla_attention(q, k, vt, qsq, ksq, seq):
    t = q.shape[0]
    bsz = t // seq
    sq_spec = pl.BlockSpec((1,) + qsq.shape[1:], lambda b, h: (b, 0, 0, 0))
    return pl.pallas_call(
        _attn_body,
        grid=(bsz, MLA_HEADS),
        in_specs=[pl.BlockSpec((seq, QK_PAD), lambda b, h: (b, h)),
                  pl.BlockSpec((seq, QK_PAD), lambda b, h: (b, h)),
                  pl.BlockSpec((V_DIM, seq), lambda b, h: (b * MLA_HEADS + h, 0)),
                  sq_spec, sq_spec],
        out_specs=pl.BlockSpec((seq, V_DIM), lambda b, h: (b, h)),
        out_shape=jax.ShapeDtypeStruct((t, MLA_HEADS * V_DIM), BF16),
        scratch_shapes=[pltpu.VMEM((2, seq, ATTN_Q), F32), pltpu.VMEM((2, seq, ATTN_Q), BF16)],
        compiler_params=_params("parallel", "parallel"),
        name="mla_attn",
    )(q, k, vt, qsq, ksq)


def _memkv_body(m_ref, g_ref, wkv_ref, k_out, v_out):
    h = _rms(m_ref[...], g_ref[...]).astype(BF16)
    k_out[...] = _dot(h, wkv_ref[:, :MEM_WIDTH]).astype(BF16)
    v_out[...] = _dot(h, wkv_ref[:, MEM_WIDTH:]).astype(BF16)


def _memkv(mem, g, wkv):
    t = mem.shape[0]
    tm = MEMKV_TOKENS
    row = pl.BlockSpec((tm, D_MODEL), lambda b: (b, 0))
    out = pl.BlockSpec((tm, MEM_WIDTH), lambda b: (b, 0))
    return pl.pallas_call(
        _memkv_body,
        grid=(t // tm,),
        in_specs=[row, _const_spec(g.shape), _const_spec(wkv.shape)],
        out_specs=[out, out],
        out_shape=[jax.ShapeDtypeStruct((t, MEM_WIDTH), BF16)] * 2,
        compiler_params=_params("parallel"),
        name="memkv",
    )(mem, g, wkv)


def _merge_body(x_ref, ya_ref, yb_ref, mq_ref, gate_ref, mk_ref, mv_ref, wb_ref, wo_ref, o_ref):
    heads = [slice(hd * MEM_HEAD_DIM, (hd + 1) * MEM_HEAD_DIM) for hd in range(MEM_HEADS)]
    scores = [_dot_nt(mq_ref[:, sl], mk_ref[:, sl]) * MQ_SCALE for sl in heads]
    proj_a = _dot(ya_ref[...], wb_ref[0])
    probs = []
    for s in scores:
        p = jnp.exp2(s - jnp.max(s, axis=-1, keepdims=True))
        probs.append((p.astype(BF16), jnp.sum(p, axis=-1, keepdims=True)))
    ycs = [(_dot(p, mv_ref[:, sl]) / l).astype(BF16) for (p, l), sl in zip(probs, heads)]
    proj_b = _dot(yb_ref[...], wb_ref[1])
    proj_c = _dot(jnp.concatenate(ycs, axis=1), wb_ref[2])

    merged = None
    for n, proj in enumerate((proj_a, proj_b, proj_c)):
        term = jax.nn.sigmoid(gate_ref[:, n * D_MODEL:(n + 1) * D_MODEL].astype(F32)) * proj
        merged = term if merged is None else merged + term
    o_ref[...] = x_ref[...] + _dot(merged.astype(BF16), wo_ref[...])


def _merge(x, ya, yb, mq, gates, mk, mv, wb, wo, seq):
    t = x.shape[0]
    tm = MERGE_TOKENS
    n_s = seq // tm

    def row(width):
        return pl.BlockSpec((tm, width), lambda i: (i, 0))

    memspec = pl.BlockSpec((N_MEM, MEM_WIDTH), lambda i: (i // n_s, 0))
    return pl.pallas_call(
        _merge_body,
        grid=(t // tm,),
        in_specs=[row(D_MODEL), row(D_MODEL), row(D_MODEL), row(MEM_WIDTH), row(N_BRANCH * D_MODEL),
                  memspec, memspec, _const_spec(wb.shape), _const_spec(wo.shape)],
        out_specs=row(D_MODEL),
        out_shape=jax.ShapeDtypeStruct((t, D_MODEL), F32),
        compiler_params=_params("parallel"),
        name="merge",
    )(x, ya, yb, mq, gates, mk, mv, wb, wo)


def _rotary_tables(seq):
    pos = np.arange(seq, dtype=np.float32)
    inv_freq = np.float32(1.0) / (np.float32(ROPE_THETA) ** (np.arange(0, ROPE_DIM, 2, dtype=np.float32) / np.float32(ROPE_DIM)))
    ang = pos[:, None] * inv_freq[None, :]
    cos, sin = np.cos(ang).astype(np.float32), np.sin(ang).astype(np.float32)
    zero = np.zeros_like(cos)

    def lay(a, b):
        return np.concatenate([a, b, np.zeros((seq, LANES - 2 * HALF_ROPE), np.float32)], axis=1)

    return tuple(jnp.asarray(t) for t in (lay(cos, cos), lay(-sin, zero), lay(zero, sin)))


def _pad_heads(w, keep, per_head, pad_to):
    fan_in = w.shape[0]
    w = w.reshape(fan_in, -1, per_head)[:, :, :keep]
    w = jnp.pad(w, ((0, 0), (0, 0), (0, pad_to - keep)))
    return w.reshape(fan_in, -1)


def kernel(x, mem, ffn1_norm, ffn1_w_in, ffn1_w_down, mix_norm, w_in, b_gate, q_norm, w_uq, kv_norm, w_ukv,
           conv_w, conv_b, w_rg_a, b_rg_a, w_rg_i, b_rg_i, lru_lambda, mem_norm, w_mem_kv, w_branch, w_out,
           ffn2_norm, ffn2_w_in, ffn2_w_down, final_norm):
    bsz, seq, _ = x.shape
    depth = ffn1_norm.shape[0]
    xt = x.reshape(bsz * seq, D_MODEL)
    memt = mem.reshape(bsz * N_MEM, D_MODEL)
    tables = _rotary_tables(seq)
    fn = final_norm.reshape(1, D_MODEL)
    bf = lambda a: a.astype(BF16)

    for l in range(depth):
        last = l == depth - 1
        xt = _ffn(xt, ffn1_norm[l].reshape(1, -1), bf(ffn1_w_in[l]), bf(ffn1_w_down[l]), fn, final=False)

        win = bf(w_in[l].T)
        wuq = bf(_pad_heads(w_uq[l], NOPE_DIM + ROPE_DIM, NOPE_DIM + ROPE_DIM, QK_PAD))
        wukv = w_ukv[l].reshape(KV_LORA, MLA_HEADS, NOPE_DIM + V_DIM)
        wuk = bf(wukv[:, :, :NOPE_DIM].reshape(KV_LORA, -1))
        wuvt = bf(wukv[:, :, NOPE_DIM:].reshape(KV_LORA, -1).T)
        wai = bf(jnp.concatenate([w_rg_a[l], w_rg_i[l]], axis=-1))
        bai = jnp.concatenate([b_rg_a[l], b_rg_i[l]], axis=-1).reshape(RNN_BLOCKS, 1, 2 * RNN_BLOCK_DIM)
        consts = [mix_norm[l].reshape(1, -1), win, b_gate[l].reshape(1, -1),
                  q_norm[l].reshape(1, -1), wuq, kv_norm[l].reshape(1, -1), wuk, wuvt,
                  conv_w[l].reshape(CONV_WIDTH, RNN_WIDTH), conv_b[l].reshape(1, -1), wai, bai,
                  lru_lambda[l].reshape(1, -1)]
        q, k, vt, yb, mq, gates, qsq, ksq = _mix_in(xt, seq, consts, tables)

        ya = _mla_attention(q, k, vt, qsq, ksq, seq)
        mk, mv = _memkv(memt, mem_norm[l].reshape(1, -1), bf(w_mem_kv[l]))
        xt = _merge(xt, ya, yb, mq, gates, mk, mv, bf(w_branch[l]), bf(w_out[l]), seq)

        xt = _ffn(xt, ffn2_norm[l].reshape(1, -1), bf(ffn2_w_in[l]), bf(ffn2_w_down[l]), fn, final=last)
    return xt.reshape(bsz, seq, D_MODEL)
```

```python
import functools

import jax
import jax.numpy as jnp
import numpy as np
from jax import lax
from jax.experimental import pallas as pl
from jax.experimental.pallas import tpu as pltpu

F32 = jnp.float32
BF16 = jnp.bfloat16

D_MODEL = 1024
CHUNK = 64
N_MEM = 256
RMS_EPS = 1e-6
FFN_HIDDEN = 2816
MLA_HEADS = 8
Q_LORA = 384
KV_LORA = 256
NOPE_DIM = 128
ROPE_DIM = 64
V_DIM = 128
ROPE_THETA = 10000.0
RNN_WIDTH = 1024
RNN_BLOCKS = 8
RNN_BLOCK_DIM = RNN_WIDTH // RNN_BLOCKS
CONV_WIDTH = 4
LRU_C = 8.0
MEM_HEADS = 4
MEM_HEAD_DIM = 256
MEM_WIDTH = MEM_HEADS * MEM_HEAD_DIM
N_BRANCH = 3

LOG2E = float(np.log2(np.e))
Q_SCALE = (NOPE_DIM + ROPE_DIM) ** -0.5 * LOG2E
MQ_SCALE = MEM_HEAD_DIM ** -0.5 * LOG2E
HALF_ROPE = ROPE_DIM // 2
LANES = 128
QK_PAD = 256
LAT_WIDTH = 768
WIN_X = Q_LORA + KV_LORA + ROPE_DIM
WIN_G = WIN_X + RNN_WIDTH
WIN_MQ = WIN_G + RNN_WIDTH
WIN_GATE = WIN_MQ + MEM_WIDTH
HALO = 8

VMEM_LIMIT_BYTES = 56 * 1024 * 1024

FFN_TOKENS = 1024
FFN_PARTS = 4
FFN_CHUNKS = ((0, 1536), (1536, FFN_HIDDEN))
MIX_TOKENS = 512
MIX_GROUPS = 4
MIX_GROUP_WIDTH = RNN_WIDTH // MIX_GROUPS
SCAN_NSEG = 8
SCAN_SEG = MIX_TOKENS // SCAN_NSEG
SCAN_PITCH = SCAN_SEG + 8
ATTN_Q = 256
ATTN_AHEAD = 3
ATTN_BOUND_MARGIN = 1.05
ATTN_BOUND_LIMIT = 60.0
ATTN_SLAB = 64
MERGE_TOKENS = 512
MEMKV_TOKENS = 1024


def _dot(a, b):
    return jnp.dot(a, b, preferred_element_type=F32)


def _dot_nt(a, b):
    return lax.dot_general(a, b, (((1,), (1,)), ((), ())), preferred_element_type=F32)


def _rms(x, g):
    return x * lax.rsqrt(jnp.mean(x * x, axis=-1, keepdims=True) + RMS_EPS) * g


def _const_spec(shape):
    nd = len(shape)
    return pl.BlockSpec(shape, lambda *_: (0,) * nd, pipeline_mode=pl.Buffered(1))


def _params(*sem):
    return pltpu.CompilerParams(dimension_semantics=sem, vmem_limit_bytes=VMEM_LIMIT_BYTES)


def _ffn_body(x_ref, g_ref, wgu_ref, wd_ref, fn_ref, o_ref, *, final):
    part_rows = x_ref.shape[0] // FFN_PARTS
    for r in range(FFN_PARTS):
        rows = slice(r * part_rows, (r + 1) * part_rows)
        x = x_ref[rows, :]
        h = _rms(x, g_ref[...]).astype(BF16)
        acc = None
        for lo, hi in FFN_CHUNKS:
            gate = _dot(h, wgu_ref[:, lo:hi])
            up = _dot(h, wgu_ref[:, FFN_HIDDEN + lo:FFN_HIDDEN + hi])
            a = (gate * jax.nn.sigmoid(gate) * up).astype(BF16)
            part = _dot(a, wd_ref[lo:hi, :])
            acc = part if acc is None else acc + part
        y = x + 0.5 * acc
        if final:
            y = _rms(y, fn_ref[...])
        o_ref[rows, :] = y


def _ffn(x, g, wgu, wd, fn, *, final):
    t = x.shape[0]
    tm = FFN_TOKENS
    row = pl.BlockSpec((tm, D_MODEL), lambda i: (i, 0))
    return pl.pallas_call(
        functools.partial(_ffn_body, final=final),
        grid=(t // tm,),
        in_specs=[row, _const_spec((1, D_MODEL)), _const_spec(wgu.shape), _const_spec(wd.shape),
                  _const_spec((1, D_MODEL))],
        out_specs=row,
        out_shape=jax.ShapeDtypeStruct((t, D_MODEL), F32),
        compiler_params=_params("parallel"),
        name="ffn_final" if final else "ffn",
    )(x, g, wgu, wd, fn)


def _gelu_tanh(x):
    c = float(np.sqrt(2.0 / np.pi))
    inner = x * ((x * x) * (c * 0.044715) + c)
    return x * (0.5 * jnp.tanh(inner) + 0.5)


def _rot(x, c, s_lo, s_hi):
    w = x.shape[-1]
    return x * c + pltpu.roll(x, w - HALF_ROPE, 1) * s_lo + pltpu.roll(x, HALF_ROPE, 1) * s_hi


def _rglru_coeffs(xc, gates, n0, bai_ref, neg_c_log2, pos_c, a_s, u_s):
    for k, g in enumerate(gates):
        n = n0 + k
        loc = slice(k * RNN_BLOCK_DIM, (k + 1) * RNN_BLOCK_DIM)
        blk = slice(n * RNN_BLOCK_DIM, (n + 1) * RNN_BLOCK_DIM)
        g = g + bai_ref[n]
        r = jax.nn.sigmoid(g[:, :RNN_BLOCK_DIM])
        ig = jax.nn.sigmoid(g[:, RNN_BLOCK_DIM:])
        a = jnp.exp2(r * neg_c_log2[:, blk])
        y = jnp.tanh(r * pos_c[:, blk]) * (1.0 + a * a)
        root = jnp.where(y == 0.0, 0.0, y * lax.rsqrt(y))
        u = root * (ig * xc[:, loc])
        for s in range(SCAN_NSEG):
            src = slice(s * SCAN_SEG, (s + 1) * SCAN_SEG)
            dst = slice(s * SCAN_PITCH, s * SCAN_PITCH + SCAN_SEG)
            a_s[n, dst, :] = a[src, :]
            u_s[n, dst, :] = u[src, :]


def _segmented_scan(a_s, u_s, n, h_in):
    h_loc = prod = None
    for j in range(SCAN_SEG):
        rows = pl.ds(j, SCAN_NSEG, stride=SCAN_PITCH)
        aj, uj = a_s[n, rows, :], u_s[n, rows, :]
        h_loc = uj if j == 0 else aj * h_loc + uj
        prod = aj if j == 0 else aj * prod
        u_s[n, rows, :] = h_loc
        a_s[n, rows, :] = prod
    seg_in = []
    for s in range(SCAN_NSEG):
        seg_in.append(h_in)
        last = s * SCAN_PITCH + SCAN_SEG - 1
        h_in = a_s[n, last:last + 1, :] * h_in + u_s[n, last:last + 1, :]
    return seg_in, h_in


def _mix_in_body(x_ref, gmix_ref, win_ref, bgate_ref,
                 qnorm_ref, wuq_ref, kvnorm_ref, wuk_ref, wuvt_ref,
                 cw_ref, cb_ref, wai_ref, bai_ref, lam_ref,
                 rc_ref, rlo_ref, rhi_ref,
                 q_out, k_out, vt_out, yb_out, mq_out, gate_out, qsq_out, ksq_out,
                 xbuf, hstate, a_s, u_s):
    ts = MIX_TOKENS

    @pl.when(pl.program_id(1) == 0)
    def _():
        xbuf[0:HALO, :] = jnp.zeros((HALO, RNN_WIDTH), F32)
        hstate[...] = jnp.zeros((1, RNN_WIDTH), F32)

    h = _rms(x_ref[...], gmix_ref[...]).astype(BF16)
    lat = _dot_nt(h, win_ref[0:LAT_WIDTH, :])
    hq = _rms(lat[:, :Q_LORA], qnorm_ref[...] * Q_SCALE).astype(BF16)
    hkv = _rms(lat[:, Q_LORA:Q_LORA + KV_LORA], kvnorm_ref[...]).astype(BF16)
    rot_c, rot_lo, rot_hi = rc_ref[...], rlo_ref[...], rhi_ref[...]
    kr = _rot(lat[:, Q_LORA + KV_LORA:], rot_c, rot_lo, rot_hi)
    kr_sq = jnp.sum(kr * kr, axis=1, keepdims=True)
    kr = kr.astype(BF16)

    lam = lam_ref[...]
    sp = jnp.maximum(-lam, 0.0) + jnp.log1p(jnp.exp(-jnp.abs(lam)))
    neg_c_log2 = (-LRU_C * LOG2E) * sp
    pos_c = LRU_C * sp

    blocks = MIX_GROUP_WIDTH // RNN_BLOCK_DIM
    heads = MIX_GROUP_WIDTH // NOPE_DIM
    for grp in range(MIX_GROUPS):
        c0, c1 = grp * MIX_GROUP_WIDTH, (grp + 1) * MIX_GROUP_WIDTH
        cols = slice(c0, c1)
        gcols = slice(grp * N_BRANCH * MIX_GROUP_WIDTH, (grp + 1) * N_BRANCH * MIX_GROUP_WIDTH)
        n0 = grp * blocks

        xbuf[HALO:HALO + ts, cols] = _dot_nt(h, win_ref[WIN_X + c0:WIN_X + c1, :])
        q = _dot(hq, wuq_ref[:, grp * heads * QK_PAD:(grp + 1) * heads * QK_PAD])
        kn = _dot(hkv, wuk_ref[:, cols])
        vt_out[cols, :] = _dot_nt(wuvt_ref[cols, :], hkv).astype(BF16)

        xc = cb_ref[:, cols] + cw_ref[CONV_WIDTH - 1:CONV_WIDTH, cols] * xbuf[HALO:HALO + ts, cols]
        for d in range(1, CONV_WIDTH):
            xc = xc + cw_ref[CONV_WIDTH - 1 - d:CONV_WIDTH - d, cols] * xbuf[HALO - d:HALO - d + ts, cols]
        xbuf[0:HALO, cols] = xbuf[ts:ts + HALO, cols]
        xcb = xc.astype(BF16)
        gates = [_dot(xcb[:, k * RNN_BLOCK_DIM:(k + 1) * RNN_BLOCK_DIM], wai_ref[n0 + k]) for k in range(blocks)]

        mq = _dot_nt(h, win_ref[WIN_MQ + c0:WIN_MQ + c1, :])
        gz = _dot_nt(h, win_ref[WIN_G + c0:WIN_G + c1, :])
        merge_gate = _dot_nt(h, win_ref[WIN_GATE + N_BRANCH * c0:WIN_GATE + N_BRANCH * c1, :])

        _rglru_coeffs(xc, gates, n0, bai_ref, neg_c_log2, pos_c, a_s, u_s)
        gz = _gelu_tanh(gz)
        for k in range(blocks):
            n = n0 + k
            blk = slice(n * RNN_BLOCK_DIM, (n + 1) * RNN_BLOCK_DIM)
            seg_in, h_last = _segmented_scan(a_s, u_s, n, hstate[:, blk])
            hstate[:, blk] = h_last
            for s, h_seg in enumerate(seg_in):
                rows = slice(s * SCAN_SEG, (s + 1) * SCAN_SEG)
                pad = slice(s * SCAN_PITCH, s * SCAN_PITCH + SCAN_SEG)
                h_true = u_s[n, pad, :] + a_s[n, pad, :] * h_seg
                yb_out[rows, blk] = (h_true * gz[rows, k * RNN_BLOCK_DIM:(k + 1) * RNN_BLOCK_DIM]).astype(BF16)

        for k in range(heads):
            hd = grp * heads + k
            base = hd * QK_PAD
            q_nope = q[:, k * QK_PAD:k * QK_PAD + NOPE_DIM]
            q_rope = _rot(q[:, k * QK_PAD + NOPE_DIM:(k + 1) * QK_PAD], rot_c, rot_lo, rot_hi)
            k_nope = kn[:, k * NOPE_DIM:(k + 1) * NOPE_DIM]
            q_out[:, base:base + NOPE_DIM] = q_nope.astype(BF16)
            q_out[:, base + NOPE_DIM:base + QK_PAD] = q_rope.astype(BF16)
            k_out[:, base:base + NOPE_DIM] = k_nope.astype(BF16)
            k_out[:, base + NOPE_DIM:base + QK_PAD] = kr
            q_sq = jnp.sum(q_nope * q_nope + q_rope * q_rope, axis=1, keepdims=True)
            k_sq = jnp.sum(k_nope * k_nope, axis=1, keepdims=True) + kr_sq
            qsq_out[0, 0, hd:hd + 1, :] = jnp.broadcast_to(jnp.max(q_sq, axis=0, keepdims=True), (1, LANES))
            ksq_out[0, 0, hd:hd + 1, :] = jnp.broadcast_to(jnp.max(k_sq, axis=0, keepdims=True), (1, LANES))
        mq_out[:, cols] = mq.astype(BF16)
        gate_out[:, gcols] = (merge_gate + bgate_ref[:, gcols]).astype(BF16)


def _mix_in(x, seq, consts, tables):
    t = x.shape[0]
    tm = MIX_TOKENS
    n_s = seq // tm
    bsz = t // seq

    def row(width):
        return pl.BlockSpec((tm, width), lambda b, i: (b * n_s + i, 0))

    def tab(width):
        return pl.BlockSpec((tm, width), lambda b, i: (i, 0))

    row_outs = [(MLA_HEADS * QK_PAD, BF16), (MLA_HEADS * QK_PAD, BF16)]
    tail_outs = [(RNN_WIDTH, BF16), (MEM_WIDTH, BF16), (N_BRANCH * D_MODEL, BF16)]
    vt_spec = pl.BlockSpec((MLA_HEADS * V_DIM, tm), lambda b, i: (b, i))
    vt_shape = jax.ShapeDtypeStruct((bsz * MLA_HEADS * V_DIM, seq), BF16)
    sq_spec = pl.BlockSpec((1, 1, MLA_HEADS, LANES), lambda b, i: (b, i, 0, 0))
    sq_shape = jax.ShapeDtypeStruct((bsz, n_s, MLA_HEADS, LANES), F32)
    return pl.pallas_call(
        _mix_in_body,
        grid=(bsz, n_s),
        in_specs=[row(D_MODEL)] + [_const_spec(c.shape) for c in consts] + [tab(LANES)] * 3,
        out_specs=[row(w) for w, _ in row_outs] + [vt_spec] + [row(w) for w, _ in tail_outs] + [sq_spec] * 2,
        out_shape=[jax.ShapeDtypeStruct((t, w), dt) for w, dt in row_outs] + [vt_shape]
                  + [jax.ShapeDtypeStruct((t, w), dt) for w, dt in tail_outs] + [sq_shape] * 2,
        scratch_shapes=[pltpu.VMEM((HALO + tm, RNN_WIDTH), F32), pltpu.VMEM((1, RNN_WIDTH), F32),
                        pltpu.VMEM((RNN_BLOCKS, SCAN_NSEG * SCAN_PITCH, RNN_BLOCK_DIM), F32),
                        pltpu.VMEM((RNN_BLOCKS, SCAN_NSEG * SCAN_PITCH, RNN_BLOCK_DIM), F32)],
        compiler_params=_params("parallel", "arbitrary"),
        name="mix_in",
    )(x, *consts, *tables)


def _col_reduce(op, x):
    n, w = x.shape
    return op(op(x.reshape(n // ATTN_SLAB, ATTN_SLAB, w), axis=0), axis=0, keepdims=True)


def _attn_body(q_ref, k_ref, vt_ref, qsq_ref, ksq_ref, o_ref, s_buf, p_buf):
    seq = q_ref.shape[0]
    tq = ATTN_Q
    key_chunk = lax.broadcasted_iota(jnp.int32, (tq, tq), 0) // CHUNK
    qry_chunk = lax.broadcasted_iota(jnp.int32, (tq, tq), 1) // CHUNK
    diag_mask = key_chunk <= qry_chunk
    order = list(reversed(range(seq // tq)))

    def finish(i, l):
        lo, hi = i * tq, (i + 1) * tq
        acc = _dot(vt_ref[:, 0:hi], p_buf[i % 2, 0:hi, :])
        o_ref[lo:hi, :] = (acc / l).T.astype(BF16)

    head_row = lax.broadcasted_iota(jnp.int32, (MLA_HEADS, LANES), 0) == pl.program_id(1)

    def head_max(ref):
        vals = jnp.max(ref[0], axis=0)
        return jnp.max(jnp.where(head_row, vals, 0.0), axis=0, keepdims=True)

    bound = jnp.sqrt(head_max(qsq_ref) * head_max(ksq_ref)) * ATTN_BOUND_MARGIN
    small = bound[0, 0] <= ATTN_BOUND_LIMIT

    @pl.when(small)
    def _():
        shift = jnp.concatenate([bound] * (tq // LANES), axis=1)
        blocks = [(i, kb) for i in order for kb in range(i + 1)]

        def score(i, kb):
            st = _dot_nt(k_ref[kb * tq:(kb + 1) * tq, :], q_ref[i * tq:(i + 1) * tq, :])
            return jnp.where(diag_mask, st, -jnp.inf) if kb == i else st

        pending = [score(*blk) for blk in blocks[:ATTN_AHEAD]]
        l = None
        for n, (i, kb) in enumerate(blocks):
            if n + ATTN_AHEAD < len(blocks):
                pending.append(score(*blocks[n + ATTN_AHEAD]))
            pt = jnp.exp2(pending.pop(0) - shift)
            part = _col_reduce(jnp.sum, pt)
            l = part if kb == 0 else l + part
            p_buf[i % 2, kb * tq:(kb + 1) * tq, :] = pt.astype(BF16)
            if kb == i:
                finish(i, l)

    @pl.when(jnp.logical_not(small))
    def _():
        def scores(i):
            lo, hi = i * tq, (i + 1) * tq
            q = q_ref[lo:hi, :]
            s_buf[i % 2, lo:hi, :] = jnp.where(diag_mask, _dot_nt(k_ref[lo:hi, :], q), -jnp.inf)
            for kb in range(i):
                s_buf[i % 2, kb * tq:(kb + 1) * tq, :] = _dot_nt(k_ref[kb * tq:(kb + 1) * tq, :], q)

        def softmax_pv(i):
            hi = (i + 1) * tq
            m = _col_reduce(jnp.max, s_buf[i % 2, 0:hi, :])
            pt = jnp.exp2(s_buf[i % 2, 0:hi, :] - m)
            l = _col_reduce(jnp.sum, pt)
            p_buf[i % 2, 0:hi, :] = pt.astype(BF16)
            finish(i, l)

        scores(order[0])
        for pos, i in enumerate(order):
            if pos + 1 < len(order):
                scores(order[pos + 1])
            softmax_pv(i)


def _mla_attention(q, k, vt, qsq, ksq, seq):
    t = q.shape[0]
    bsz = t // seq
    sq_spec = pl.BlockSpec((1,) + qsq.shape[1:], lambda b, h: (b, 0, 0, 0))
    return pl.pallas_call(
        _attn_body,
        grid=(bsz, MLA_HEADS),
        in_specs=[pl.BlockSpec((seq, QK_PAD), lambda b, h: (b, h)),
                  pl.BlockSpec((seq, QK_PAD), lambda b, h: (b, h)),
                  pl.BlockSpec((V_DIM, seq), lambda b, h: (b * MLA_HEADS + h, 0)),
                  sq_spec, sq_spec],
        out_specs=pl.BlockSpec((seq, V_DIM), lambda b, h: (b, h)),
        out_shape=jax.ShapeDtypeStruct((t, MLA_HEADS * V_DIM), BF16),
        scratch_shapes=[pltpu.VMEM((2, seq, ATTN_Q), F32), pltpu.VMEM((2, seq, ATTN_Q), BF16)],
        compiler_params=_params("parallel", "parallel"),
        name="mla_attn",
    )(q, k, vt, qsq, ksq)


def _memkv_body(m_ref, g_ref, wkv_ref, k_out, v_out):
    h = _rms(m_ref[...], g_ref[...]).astype(BF16)
    k_out[...] = _dot(h, wkv_ref[:, :MEM_WIDTH]).astype(BF16)
    v_out[...] = _dot(h, wkv_ref[:, MEM_WIDTH:]).astype(BF16)


def _memkv(mem, g, wkv):
    t = mem.shape[0]
    tm = MEMKV_TOKENS
    row = pl.BlockSpec((tm, D_MODEL), lambda b: (b, 0))
    out = pl.BlockSpec((tm, MEM_WIDTH), lambda b: (b, 0))
    return pl.pallas_call(
        _memkv_body,
        grid=(t // tm,),
        in_specs=[row, _const_spec(g.shape), _const_spec(wkv.shape)],
        out_specs=[out, out],
        out_shape=[jax.ShapeDtypeStruct((t, MEM_WIDTH), BF16)] * 2,
        compiler_params=_params("parallel"),
        name="memkv",
    )(mem, g, wkv)


def _merge_body(x_ref, ya_ref, yb_ref, mq_ref, gate_ref, mk_ref, mv_ref, wb_ref, wo_ref, o_ref):
    heads = [slice(hd * MEM_HEAD_DIM, (hd + 1) * MEM_HEAD_DIM) for hd in range(MEM_HEADS)]
    scores = [_dot_nt(mq_ref[:, sl], mk_ref[:, sl]) * MQ_SCALE for sl in heads]
    proj_a = _dot(ya_ref[...], wb_ref[0])
    probs = []
    for s in scores:
        p = jnp.exp2(s - jnp.max(s, axis=-1, keepdims=True))
        probs.append((p.astype(BF16), jnp.sum(p, axis=-1, keepdims=True)))
    ycs = [(_dot(p, mv_ref[:, sl]) / l).astype(BF16) for (p, l), sl in zip(probs, heads)]
    proj_b = _dot(yb_ref[...], wb_ref[1])
    proj_c = _dot(jnp.concatenate(ycs, axis=1), wb_ref[2])

    merged = None
    for n, proj in enumerate((proj_a, proj_b, proj_c)):
        term = jax.nn.sigmoid(gate_ref[:, n * D_MODEL:(n + 1) * D_MODEL].astype(F32)) * proj
        merged = term if merged is None else merged + term
    o_ref[...] = x_ref[...] + _dot(merged.astype(BF16), wo_ref[...])


def _merge(x, ya, yb, mq, gates, mk, mv, wb, wo, seq):
    t = x.shape[0]
    tm = MERGE_TOKENS
    n_s = seq // tm

    def row(width):
        return pl.BlockSpec((tm, width), lambda i: (i, 0))

    memspec = pl.BlockSpec((N_MEM, MEM_WIDTH), lambda i: (i // n_s, 0))
    return pl.pallas_call(
        _merge_body,
        grid=(t // tm,),
        in_specs=[row(D_MODEL), row(D_MODEL), row(D_MODEL), row(MEM_WIDTH), row(N_BRANCH * D_MODEL),
                  memspec, memspec, _const_spec(wb.shape), _const_spec(wo.shape)],
        out_specs=row(D_MODEL),
        out_shape=jax.ShapeDtypeStruct((t, D_MODEL), F32),
        compiler_params=_params("parallel"),
        name="merge",
    )(x, ya, yb, mq, gates, mk, mv, wb, wo)


def _rotary_tables(seq):
    pos = np.arange(seq, dtype=np.float32)
    inv_freq = np.float32(1.0) / (np.float32(ROPE_THETA) ** (np.arange(0, ROPE_DIM, 2, dtype=np.float32) / np.float32(ROPE_DIM)))
    ang = pos[:, None] * inv_freq[None, :]
    cos, sin = np.cos(ang).astype(np.float32), np.sin(ang).astype(np.float32)
    zero = np.zeros_like(cos)

    def lay(a, b):
        return np.concatenate([a, b, np.zeros((seq, LANES - 2 * HALF_ROPE), np.float32)], axis=1)

    return tuple(jnp.asarray(t) for t in (lay(cos, cos), lay(-sin, zero), lay(zero, sin)))


def _pad_heads(w, keep, per_head, pad_to):
    fan_in = w.shape[0]
    w = w.reshape(fan_in, -1, per_head)[:, :, :keep]
    w = jnp.pad(w, ((0, 0), (0, 0), (0, pad_to - keep)))
    return w.reshape(fan_in, -1)


def kernel(x, mem, ffn1_norm, ffn1_w_in, ffn1_w_down, mix_norm, w_in, b_gate, q_norm, w_uq, kv_norm, w_ukv,
           conv_w, conv_b, w_rg_a, b_rg_a, w_rg_i, b_rg_i, lru_lambda, mem_norm, w_mem_kv, w_branch, w_out,
           ffn2_norm, ffn2_w_in, ffn2_w_down, final_norm):
    bsz, seq, _ = x.shape
    depth = ffn1_norm.shape[0]
    xt = x.reshape(bsz * seq, D_MODEL)
    memt = mem.reshape(bsz * N_MEM, D_MODEL)
    tables = _rotary_tables(seq)
    fn = final_norm.reshape(1, D_MODEL)
    bf = lambda a: a.astype(BF16)

    for l in range(depth):
        last = l == depth - 1
        xt = _ffn(xt, ffn1_norm[l].reshape(1, -1), bf(ffn1_w_in[l]), bf(ffn1_w_down[l]), fn, final=False)

        win = bf(w_in[l].T)
        wuq = bf(_pad_heads(w_uq[l], NOPE_DIM + ROPE_DIM, NOPE_DIM + ROPE_DIM, QK_PAD))
        wukv = w_ukv[l].reshape(KV_LORA, MLA_HEADS, NOPE_DIM + V_DIM)
        wuk = bf(wukv[:, :, :NOPE_DIM].reshape(KV_LORA, -1))
        wuvt = bf(wukv[:, :, NOPE_DIM:].reshape(KV_LORA, -1).T)
        wai = bf(jnp.concatenate([w_rg_a[l], w_rg_i[l]], axis=-1))
        bai = jnp.concatenate([b_rg_a[l], b_rg_i[l]], axis=-1).reshape(RNN_BLOCKS, 1, 2 * RNN_BLOCK_DIM)
        consts = [mix_norm[l].reshape(1, -1), win, b_gate[l].reshape(1, -1),
                  q_norm[l].reshape(1, -1), wuq, kv_norm[l].reshape(1, -1), wuk, wuvt,
                  conv_w[l].reshape(CONV_WIDTH, RNN_WIDTH), conv_b[l].reshape(1, -1), wai, bai,
                  lru_lambda[l].reshape(1, -1)]
        q, k, vt, yb, mq, gates, qsq, ksq = _mix_in(xt, seq, consts, tables)

        ya = _mla_attention(q, k, vt, qsq, ksq, seq)
        mk, mv = _memkv(memt, mem_norm[l].reshape(1, -1), bf(w_mem_kv[l]))
        xt = _merge(xt, ya, yb, mq, gates, mk, mv, bf(w_branch[l]), bf(w_out[l]), seq)

        xt = _ffn(xt, ffn2_norm[l].reshape(1, -1), bf(ffn2_w_in[l]), bf(ffn2_w_down[l]), fn, final=last)
    return xt.reshape(bsz, seq, D_MODEL)
```
